```python
import math
import jax
import jax.numpy as jnp
from jax import lax
import numpy as np

D_MODEL = 4096
BATCH = 4
SEQ = 2048
DEPTH = 2
DEC_BATCH = 128
DEC_SEQ = 4
PAST_LEN = 16384
PAGE_SIZE = 128

MLA_HEADS = 16
Q_LORA = 1024
KV_LORA = 512
NOPE_DIM = 128
ROPE_DIM = 64
MLA_V_DIM = 128
ROPE_THETA = 10000.0
MLA_SCALE = (NOPE_DIM + ROPE_DIM) ** -0.5
MLSTM_HEADS = 4
MLSTM_DK = 256
MLSTM_DV = 512
MLSTM_CHUNK = 64
FOX_HEADS = 32
FOX_KV_HEADS = 2
FOX_GROUP = FOX_HEADS // FOX_KV_HEADS
FOX_HEAD_DIM = 128
FOX_SCALE = FOX_HEAD_DIM ** -0.5
FORGET_BIAS = 3.0
N_GROUPS = 4
EXPERTS_PER_GROUP = 8
N_EXPERTS = N_GROUPS * EXPERTS_PER_GROUP
TOP_K_IN_GROUP = 2
EXPERT_DIM = 512
ROUTER_BIAS_SCALE = 0.01
Q_BLOCK = 128
N_AB_LAYERS = (DEPTH + 1) // 2
N_C_LAYERS = DEPTH // 2
DEEPNORM_ALPHA = (2.0 * DEPTH) ** 0.25
DEEPNORM_BETA = (8.0 * DEPTH) ** -0.25
LN_EPS = 1e-5
RMS_EPS = 1e-6
AB_SIZES = (Q_LORA, KV_LORA, ROPE_DIM, MLSTM_HEADS * MLSTM_DK, MLSTM_HEADS * MLSTM_DK, MLSTM_HEADS * MLSTM_DV, MLSTM_HEADS * MLSTM_DV, MLSTM_HEADS, MLSTM_HEADS)
AB_IN = sum(AB_SIZES)
AB_MIX = MLA_HEADS * MLA_V_DIM + MLSTM_HEADS * MLSTM_DV
FOX_SIZES = (FOX_HEADS * FOX_HEAD_DIM, FOX_KV_HEADS * FOX_HEAD_DIM, FOX_KV_HEADS * FOX_HEAD_DIM, FOX_HEADS)
FOX_IN = sum(FOX_SIZES)
FOX_MIX = FOX_HEADS * FOX_HEAD_DIM
F32 = jnp.float32

kernel_name = 'hybrid_mla_mlstm_fox_hmoe_decode_step'


def _split(z, sizes):
    cuts = np.cumsum(np.array(sizes))[:-1].tolist()
    return jnp.split(z, cuts, axis=-1)


def layer_norm(x, g, b):
    xf = x.astype(F32)
    mu = xf.mean(-1, keepdims=True)
    var = jnp.square(xf - mu).mean(-1, keepdims=True)
    return ((xf - mu) * lax.rsqrt(var + LN_EPS) * g.astype(F32) + b.astype(F32)).astype(x.dtype)


def rms_norm(x, g):
    xf = x.astype(F32)
    return (xf * lax.rsqrt(jnp.square(xf).mean(-1, keepdims=True) + RMS_EPS) * g.astype(F32)).astype(x.dtype)


def head_layer_norm(h, g):
    mu = h.mean(-1, keepdims=True)
    var = jnp.square(h - mu).mean(-1, keepdims=True)
    return (h - mu) * lax.rsqrt(var + LN_EPS) * g.astype(F32)


def deepnorm_residual(x, h, g, b):
    return layer_norm(DEEPNORM_ALPHA * x + h, g, b)


def apply_rope(x, pos):
    half = x.shape[-1] // 2
    inv = ROPE_THETA ** (-jnp.arange(half, dtype=F32) / half)
    ang = pos[:, None] * inv[None, :]
    cos = jnp.cos(ang)[None, :, None, :]
    sin = jnp.sin(ang)[None, :, None, :]
    xf = x.astype(F32)
    x1, x2 = xf[..., :half], xf[..., half:]
    return jnp.concatenate([x1 * cos - x2 * sin, x2 * cos + x1 * sin], axis=-1).astype(x.dtype)


def online_softmax_step(carry, s, v, eq):
    m, l, acc = carry
    m_new = jnp.maximum(m, s.max(-1))
    a = jnp.exp(m - m_new)
    p = jnp.exp(s - m_new[..., None])
    return (m_new, a * l + p.sum(-1), a[..., None] * acc + jnp.einsum(eq, p, v))


def blocked_causal_attention(q, k, v, fcum, scale):
    B, T, KH, G, D = q.shape
    nb = T // Q_BLOCK
    qb = jnp.moveaxis(q.reshape(B, nb, Q_BLOCK, KH, G, D), 1, 0)
    key_pos = jnp.arange(T)
    fk = None if fcum is None else jnp.transpose(fcum, (0, 2, 3, 1))[:, :, :, None, :]

    def one_block(args):
        i, qi = args[0], args[1]
        s = jnp.einsum('bqkgd,bskd->bkgqs', qi, k, preferred_element_type=F32) * scale
        if fcum is not None:
            s = s + jnp.transpose(args[2], (0, 2, 3, 1))[..., None] - fk
        q_pos = i * Q_BLOCK + jnp.arange(Q_BLOCK)
        s = jnp.where(key_pos[None, :] <= q_pos[:, None], s, -jnp.inf)
        p = jax.nn.softmax(s, axis=-1)
        return jnp.einsum('bkgqs,bske->bqkge', p.astype(v.dtype), v)

    if fcum is None:
        xs = (jnp.arange(nb), qb)
    else:
        xs = (jnp.arange(nb), qb, jnp.moveaxis(fcum.reshape(B, nb, Q_BLOCK, KH, G), 1, 0))
    out = lax.map(one_block, xs)
    return jnp.moveaxis(out, 0, 1).reshape(B, T, KH, G, v.shape[-1])


def mlstm_chunkwise(q, k, v, i_pre, logf, c0, n0, m0):
    B, T, H, _ = q.shape
    L = T if T <= MLSTM_CHUNK else MLSTM_CHUNK
    nc = T // L
    causal = jnp.tril(jnp.ones((L, L), dtype=bool))

    def to_chunks(a):
        a = a.astype(F32).reshape((B, nc, L) + a.shape[2:])
        return jnp.swapaxes(jnp.moveaxis(a, 1, 0), 2, 3)

    def chunk_step(carry, xs):
        c, n, m = carry
        qc, kc, vc, ic, fc = xs
        b = jnp.cumsum(fc, axis=-1)
        log_d = jnp.where(causal, b[..., :, None] - b[..., None, :] + ic[..., None, :], -jnp.inf)
        m_inter = b + m[..., None]
        m_t = jnp.maximum(m_inter, log_d.max(-1))
        s = jnp.einsum('bhtd,bhsd->bhts', qc, kc) * jnp.exp(log_d - m_t[..., None])
        inter = jnp.exp(m_inter - m_t)
        num = jnp.einsum('bhts,bhsv->bhtv', s, vc) + inter[..., None] * jnp.einsum('bhtd,bhdv->bhtv', qc, c)
        den = s.sum(-1) + inter * jnp.einsum('bhtd,bhd->bht', qc, n)
        h = num / jnp.maximum(jnp.abs(den), jnp.exp(-m_t))[..., None]
        m_new = m_t[..., -1]
        w = jnp.exp(b[..., -1:] - b + ic - m_new[..., None])
        decay = jnp.exp(b[..., -1] + m - m_new)
        c_new = decay[..., None, None] * c + jnp.einsum('bhs,bhsd,bhsv->bhdv', w, kc, vc)
        n_new = decay[..., None] * n + jnp.einsum('bhs,bhsd->bhd', w, kc)
        return (c_new, n_new, m_new), h

    xs = tuple(to_chunks(a) for a in (q, k, v, i_pre, logf))
    (c, n, m), h = lax.scan(chunk_step, (c0.astype(F32), n0.astype(F32), m0.astype(F32)), xs)
    h = jnp.swapaxes(jnp.moveaxis(h, 0, 1), 2, 3).reshape(B, T, H, v.shape[-1])
    return h, c, n, m


def ab_project(x, pos, w):
    B, T, _ = x.shape
    zq, zkv, zkr, mq, mk, mv, mo, mi, mf = _split(x @ w['w_in'], AB_SIZES)
    q = (rms_norm(zq, w['q_norm']) @ w['w_q_up']).reshape(B, T, MLA_HEADS, NOPE_DIM + ROPE_DIM)
    q_nope = q[..., :NOPE_DIM]
    q_rope = apply_rope(q[..., NOPE_DIM:], pos)
    c_kv = rms_norm(zkv, w['kv_norm'])
    k_rope = apply_rope(zkr[:, :, None, :], pos)[:, :, 0, :]
    m_q = mq.reshape(B, T, MLSTM_HEADS, MLSTM_DK)
    m_k = mk.reshape(B, T, MLSTM_HEADS, MLSTM_DK) * (MLSTM_DK ** -0.5)
    m_v = mv.reshape(B, T, MLSTM_HEADS, MLSTM_DV)
    m_o = jax.nn.sigmoid(mo)
    i_pre = mi.astype(F32) + w['b_i'].astype(F32)
    logf = jax.nn.log_sigmoid(mf.astype(F32) + w['b_f'].astype(F32))
    return q_nope, q_rope, c_kv, k_rope, (m_q, m_k, m_v, m_o, i_pre, logf)


def ab_output(x, o_mla, h_m, m_o, w):
    B, T, _ = x.shape
    hm = head_layer_norm(h_m, w['head_norm']) * m_o.reshape(B, T, MLSTM_HEADS, MLSTM_DV).astype(F32)
    mix = jnp.concatenate([o_mla.reshape(B, T, -1).astype(x.dtype), hm.reshape(B, T, -1).astype(x.dtype)], axis=-1)
    return mix @ w['w_out']


def ab_mixer_prompt(x, pos, w):
    B, T, _ = x.shape
    q_nope, q_rope, c_kv, k_rope, (mq, mk, mv, mo, i_pre, logf) = ab_project(x, pos, w)
    k_nope = jnp.einsum('btc,hcn->bthn', c_kv, w['w_uk'])
    v = jnp.einsum('btc,hce->bthe', c_kv, w['w_uv'])
    q = jnp.concatenate([q_nope, q_rope], axis=-1)[:, :, :, None, :]
    k = jnp.concatenate([k_nope, jnp.broadcast_to(k_rope[:, :, None, :], (B, T, MLA_HEADS, ROPE_DIM))], axis=-1)
    o_mla = blocked_causal_attention(q, k, v, None, MLA_SCALE)[:, :, :, 0, :]
    c0 = jnp.zeros((B, MLSTM_HEADS, MLSTM_DK, MLSTM_DV), F32)
    n0 = jnp.zeros((B, MLSTM_HEADS, MLSTM_DK), F32)
    m0 = jnp.zeros((B, MLSTM_HEADS), F32)
    h_m, c, n, m = mlstm_chunkwise(mq, mk, mv, i_pre, logf, c0, n0, m0)
    return ab_output(x, o_mla, h_m, mo, w), (c_kv, k_rope, c, n, m)


def mla_paged_attention(q_nope, q_rope, c_new, kr_new, pool_lat, pool_rope, layer, page_table, w_uk, w_uv):
    B, S, H, _ = q_nope.shape
    q_lat = jnp.einsum('bshn,hcn->bshc', q_nope, w_uk).astype(F32)
    q_r = q_rope.astype(F32)
    eq = 'bsht,btc->bshc'

    def scores(c, r):
        return (jnp.einsum('bshc,btc->bsht', q_lat, c.astype(F32)) + jnp.einsum('bshr,btr->bsht', q_r, r.astype(F32))) * MLA_SCALE

    causal = jnp.tril(jnp.ones((S, S), dtype=bool))
    carry = (jnp.full((B, S, H), -jnp.inf, F32), jnp.zeros((B, S, H), F32), jnp.zeros((B, S, H, KV_LORA), F32))
    s_new = jnp.where(causal[None, :, None, :], scores(c_new, kr_new), -jnp.inf)
    carry = online_softmax_step(carry, s_new, c_new.astype(F32), eq)

    def page_step(carry, ids):
        c = pool_lat[layer, ids]
        r = pool_rope[layer, ids]
        return online_softmax_step(carry, scores(c, r), c.astype(F32), eq), None

    (_, l, acc), _ = lax.scan(page_step, carry, page_table.T)
    return jnp.einsum('bshc,hce->bshe', acc / l[..., None], w_uv.astype(F32))


def ab_mixer_sample(x, pos, pool_lat, pool_rope, c0, n0, m0, page_table, layer, w):
    q_nope, q_rope, c_kv, k_rope, (mq, mk, mv, mo, i_pre, logf) = ab_project(x, pos, w)
    o_mla = mla_paged_attention(q_nope, q_rope, c_kv, k_rope, pool_lat, pool_rope, layer, page_table, w['w_uk'], w['w_uv'])
    h_m, c, n, m = mlstm_chunkwise(mq, mk, mv, i_pre, logf, c0, n0, m0)
    return ab_output(x, o_mla, h_m, mo, w), (c_kv, k_rope, c, n, m)


def fox_project(x, w):
    B, T, _ = x.shape
    zq, zk, zv, zf = _split(x @ w['w_in'], FOX_SIZES)
    q = zq.reshape(B, T, FOX_KV_HEADS, FOX_GROUP, FOX_HEAD_DIM)
    k = zk.reshape(B, T, FOX_KV_HEADS, FOX_HEAD_DIM)
    v = zv.reshape(B, T, FOX_KV_HEADS, FOX_HEAD_DIM)
    logf = jax.nn.log_sigmoid(zf.astype(F32) + w['b_f'].astype(F32))
    return q, k, v, logf


def fox_mixer_prompt(x, w):
    B, T, _ = x.shape
    q, k, v, logf = fox_project(x, w)
    fcum = jnp.cumsum(logf, axis=1).reshape(B, T, FOX_KV_HEADS, FOX_GROUP)
    o = blocked_causal_attention(q, k, v, fcum, FOX_SCALE)
    return o.reshape(B, T, FOX_MIX).astype(x.dtype) @ w['w_out'], (k, v, logf)


def fox_mixer_sample(x, pool_k, pool_v, pool_logf, page_table, layer, w):
    B, S, _ = x.shape
    n_pages = page_table.shape[1]
    q, k, v, logf = fox_project(x, w)
    qf = q.astype(F32)
    f_new = jnp.cumsum(logf, axis=1).reshape(B, S, FOX_KV_HEADS, FOX_GROUP)
    past_logf = pool_logf[layer, page_table].astype(F32).reshape(B, n_pages * PAGE_SIZE, FOX_HEADS)
    suffix = lax.cumsum(past_logf, axis=1, reverse=True)
    suffix = jnp.concatenate([suffix[:, 1:], jnp.zeros_like(suffix[:, :1])], axis=1)
    suffix = jnp.moveaxis(suffix.reshape(B, n_pages, PAGE_SIZE, FOX_KV_HEADS, FOX_GROUP), 1, 0)
    eq = 'bskgt,btke->bskge'

    def scores(kb):
        return jnp.einsum('bskgd,btkd->bskgt', qf, kb.astype(F32)) * FOX_SCALE

    causal = jnp.tril(jnp.ones((S, S), dtype=bool))
    s_new = scores(k) + f_new[..., None] - jnp.moveaxis(f_new, 1, -1)[:, None]
    s_new = jnp.where(causal[None, :, None, None, :], s_new, -jnp.inf)
    carry = (jnp.full((B, S, FOX_KV_HEADS, FOX_GROUP), -jnp.inf, F32), jnp.zeros((B, S, FOX_KV_HEADS, FOX_GROUP), F32), jnp.zeros((B, S, FOX_KV_HEADS, FOX_GROUP, FOX_HEAD_DIM), F32))
    carry = online_softmax_step(carry, s_new, v.astype(F32), eq)

    def page_step(carry, xs):
        ids, suf = xs
        kb = pool_k[layer, ids]
        vb = pool_v[layer, ids]
        s = scores(kb) + f_new[..., None] + jnp.moveaxis(suf, 1, -1)[:, None]
        return online_softmax_step(carry, s, vb.astype(F32), eq), None

    (_, l, acc), _ = lax.scan(page_step, carry, (page_table.T, suffix))
    o = (acc / l[..., None]).reshape(B, S, FOX_MIX).astype(x.dtype)
    return o @ w['w_out'], (k, v, logf)


def hier_moe(x, w):
    B, T, D = x.shape
    xt = x.reshape(B * T, D)
    g_logits = (xt @ w['w_group']).astype(F32) + w['b_group'].astype(F32)
    g_idx = jnp.argmax(g_logits, axis=-1)
    g_w = jax.nn.softmax(g_logits, axis=-1).max(-1)
    e_all = jnp.einsum('nd,gde->nge', xt, w['w_router']).astype(F32) + w['b_router'].astype(F32)
    e_logits = jnp.einsum('nge,ng->ne', e_all, jax.nn.one_hot(g_idx, N_GROUPS, dtype=F32))
    top_v, top_i = lax.top_k(e_logits, TOP_K_IN_GROUP)
    gates = g_w[:, None] * jax.nn.softmax(top_v, axis=-1)
    expert_id = g_idx[:, None] * EXPERTS_PER_GROUP + top_i
    combine = jnp.einsum('nk,nke->ne', gates, jax.nn.one_hot(expert_id, N_EXPERTS, dtype=F32))
    h_g = jnp.einsum('nd,edf->enf', xt, w['w_gate'])
    h_u = jnp.einsum('nd,edf->enf', xt, w['w_up'])
    act = jax.nn.silu(h_g) * h_u * combine.T[:, :, None].astype(x.dtype)
    return jnp.einsum('enf,efd->nd', act, w['w_down']).reshape(B, T, D)


def setup_inputs(seed: int = 0) -> dict:
    key = jax.random.key(seed)
    ks = iter(jax.random.split(key, 40))

    def nrm(shape, scale=1.0):
        return jax.random.normal(next(ks), shape, F32) * scale

    n_pages = PAST_LEN // PAGE_SIZE
    n_used = DEC_BATCH * n_pages
    n_pool = n_used + max(1, n_used // 4)
    page_table = jax.random.permutation(next(ks), n_pool)[:n_used].reshape(DEC_BATCH, n_pages).astype(jnp.int32)
    return {
        'x_prompt': nrm((BATCH, SEQ, D_MODEL)),
        'x_sample': nrm((DEC_BATCH, DEC_SEQ, D_MODEL)),
        'cache_mla_latent': nrm((N_AB_LAYERS, n_pool, PAGE_SIZE, KV_LORA)),
        'cache_mla_rope': nrm((N_AB_LAYERS, n_pool, PAGE_SIZE, ROPE_DIM)),
        'state_mlstm_c': nrm((N_AB_LAYERS, DEC_BATCH, MLSTM_HEADS, MLSTM_DK, MLSTM_DV), 0.05),
        'state_mlstm_n': nrm((N_AB_LAYERS, DEC_BATCH, MLSTM_HEADS, MLSTM_DK), 0.05),
        'state_mlstm_m': nrm((N_AB_LAYERS, DEC_BATCH, MLSTM_HEADS)),
        'cache_fox_k': nrm((N_C_LAYERS, n_pool, PAGE_SIZE, FOX_KV_HEADS, FOX_HEAD_DIM)),
        'cache_fox_v': nrm((N_C_LAYERS, n_pool, PAGE_SIZE, FOX_KV_HEADS, FOX_HEAD_DIM)),
        'cache_fox_logf': jax.nn.log_sigmoid(FORGET_BIAS + nrm((N_C_LAYERS, n_pool, PAGE_SIZE, FOX_HEADS))),
        'page_table': page_table,
        'ab_w_in': nrm((N_AB_LAYERS, D_MODEL, AB_IN), D_MODEL ** -0.5),
        'ab_q_norm': 1.0 + nrm((N_AB_LAYERS, Q_LORA), 0.02),
        'ab_kv_norm': 1.0 + nrm((N_AB_LAYERS, KV_LORA), 0.02),
        'ab_w_q_up': nrm((N_AB_LAYERS, Q_LORA, MLA_HEADS * (NOPE_DIM + ROPE_DIM)), Q_LORA ** -0.5),
        'ab_w_uk': nrm((N_AB_LAYERS, MLA_HEADS, KV_LORA, NOPE_DIM), KV_LORA ** -0.5),
        'ab_w_uv': nrm((N_AB_LAYERS, MLA_HEADS, KV_LORA, MLA_V_DIM), KV_LORA ** -0.5),
        'ab_b_i': nrm((N_AB_LAYERS, MLSTM_HEADS), 0.1),
        'ab_b_f': FORGET_BIAS + nrm((N_AB_LAYERS, MLSTM_HEADS), 0.1),
        'ab_head_norm': 1.0 + nrm((N_AB_LAYERS, MLSTM_HEADS, MLSTM_DV), 0.02),
        'ab_w_out': nrm((N_AB_LAYERS, AB_MIX, D_MODEL), AB_MIX ** -0.5 * DEEPNORM_BETA),
        'fox_w_in': nrm((N_C_LAYERS, D_MODEL, FOX_IN), D_MODEL ** -0.5),
        'fox_b_f': FORGET_BIAS + nrm((N_C_LAYERS, FOX_HEADS), 0.1),
        'fox_w_out': nrm((N_C_LAYERS, FOX_MIX, D_MODEL), FOX_MIX ** -0.5 * DEEPNORM_BETA),
        'ln1_g': 1.0 + nrm((DEPTH, D_MODEL), 0.02),
        'ln1_b': nrm((DEPTH, D_MODEL), 0.02),
        'ln2_g': 1.0 + nrm((DEPTH, D_MODEL), 0.02),
        'ln2_b': nrm((DEPTH, D_MODEL), 0.02),
        'moe_w_group': nrm((DEPTH, D_MODEL, N_GROUPS), D_MODEL ** -0.5),
        'moe_b_group': nrm((DEPTH, N_GROUPS), ROUTER_BIAS_SCALE),
        'moe_w_router': nrm((DEPTH, N_GROUPS, D_MODEL, EXPERTS_PER_GROUP), D_MODEL ** -0.5),
        'moe_b_router': nrm((DEPTH, N_GROUPS, EXPERTS_PER_GROUP), ROUTER_BIAS_SCALE),
        'moe_w_gate': nrm((DEPTH, N_EXPERTS, D_MODEL, EXPERT_DIM), D_MODEL ** -0.5),
        'moe_w_up': nrm((DEPTH, N_EXPERTS, D_MODEL, EXPERT_DIM), D_MODEL ** -0.5),
        'moe_w_down': nrm((DEPTH, N_EXPERTS, EXPERT_DIM, D_MODEL), EXPERT_DIM ** -0.5 * DEEPNORM_BETA),
    }


def reference(x_prompt, x_sample, cache_mla_latent, cache_mla_rope, state_mlstm_c, state_mlstm_n, state_mlstm_m, cache_fox_k, cache_fox_v, cache_fox_logf, page_table, ab_w_in, ab_q_norm, ab_kv_norm, ab_w_q_up, ab_w_uk, ab_w_uv, ab_b_i, ab_b_f, ab_head_norm, ab_w_out, fox_w_in, fox_b_f, fox_w_out, ln1_g, ln1_b, ln2_g, ln2_b, moe_w_group, moe_b_group, moe_w_router, moe_b_router, moe_w_gate, moe_w_up, moe_w_down):
    seq = x_prompt.shape[1]
    dec_seq = x_sample.shape[1]
    past_len = page_table.shape[1] * PAGE_SIZE
    pos_p = jnp.arange(seq, dtype=F32)
    pos_s = jnp.arange(dec_seq, dtype=F32) + past_len
    xp, xs = x_prompt, x_sample
    ab_p, ab_s = [], []
    fx_p, fx_s = [], []
    for layer in range(DEPTH):
        j = layer // 2
        if layer % 2 == 0:
            w = {'w_in': ab_w_in[j], 'q_norm': ab_q_norm[j], 'kv_norm': ab_kv_norm[j], 'w_q_up': ab_w_q_up[j], 'w_uk': ab_w_uk[j], 'w_uv': ab_w_uv[j], 'b_i': ab_b_i[j], 'b_f': ab_b_f[j], 'head_norm': ab_head_norm[j], 'w_out': ab_w_out[j]}
            hp, st_p = ab_mixer_prompt(xp, pos_p, w)
            hs, st_s = ab_mixer_sample(xs, pos_s, cache_mla_latent, cache_mla_rope, state_mlstm_c[j], state_mlstm_n[j], state_mlstm_m[j], page_table, j, w)
            ab_p.append(st_p)
            ab_s.append(st_s)
        else:
            w = {'w_in': fox_w_in[j], 'b_f': fox_b_f[j], 'w_out': fox_w_out[j]}
            hp, st_p = fox_mixer_prompt(xp, w)
            hs, st_s = fox_mixer_sample(xs, cache_fox_k, cache_fox_v, cache_fox_logf, page_table, j, w)
            fx_p.append(st_p)
            fx_s.append(st_s)
        xp = deepnorm_residual(xp, hp, ln1_g[layer], ln1_b[layer])
        xs = deepnorm_residual(xs, hs, ln1_g[layer], ln1_b[layer])
        wm = {'w_group': moe_w_group[layer], 'b_group': moe_b_group[layer], 'w_router': moe_w_router[layer], 'b_router': moe_b_router[layer], 'w_gate': moe_w_gate[layer], 'w_up': moe_w_up[layer], 'w_down': moe_w_down[layer]}
        xp = deepnorm_residual(xp, hier_moe(xp, wm), ln2_g[layer], ln2_b[layer])
        xs = deepnorm_residual(xs, hier_moe(xs, wm), ln2_g[layer], ln2_b[layer])

    def stack(group, idx, like):
        return jnp.stack([st[idx] for st in group]).astype(like.dtype)

    y_prompt, y_sample = xp, xs
    p_mla_latent = stack(ab_p, 0, cache_mla_latent)
    p_mla_rope = stack(ab_p, 1, cache_mla_rope)
    p_mlstm_c = stack(ab_p, 2, state_mlstm_c)
    p_mlstm_n = stack(ab_p, 3, state_mlstm_n)
    p_mlstm_m = stack(ab_p, 4, state_mlstm_m)
    p_fox_k = stack(fx_p, 0, cache_fox_k)
    p_fox_v = stack(fx_p, 1, cache_fox_v)
    p_fox_logf = stack(fx_p, 2, cache_fox_logf)
    s_mla_latent = stack(ab_s, 0, cache_mla_latent)
    s_mla_rope = stack(ab_s, 1, cache_mla_rope)
    s_mlstm_c = stack(ab_s, 2, state_mlstm_c)
    s_mlstm_n = stack(ab_s, 3, state_mlstm_n)
    s_mlstm_m = stack(ab_s, 4, state_mlstm_m)
    s_fox_k = stack(fx_s, 0, cache_fox_k)
    s_fox_v = stack(fx_s, 1, cache_fox_v)
    s_fox_logf = stack(fx_s, 2, cache_fox_logf)
    return (y_prompt, y_sample, p_mla_latent, p_mla_rope, p_mlstm_c, p_mlstm_n, p_mlstm_m, p_fox_k, p_fox_v, p_fox_logf, s_mla_latent, s_mla_rope, s_mlstm_c, s_mlstm_n, s_mlstm_m, s_fox_k, s_fox_v, s_fox_logf)
```

```python
import functools

import jax
import jax.numpy as jnp
from jax import lax
from jax.experimental import pallas as pl
from jax.experimental.pallas import tpu as pltpu

F32 = jnp.float32
BF16 = jnp.bfloat16
NEG_INF = float("-inf")

MLA_HEADS = 16
NOPE_DIM = 128
ROPE_DIM = 64
MLA_V_DIM = 128
Q_LORA = 1024
KV_LORA = 512
ROPE_THETA = 10000.0
MLA_SCALE = (NOPE_DIM + ROPE_DIM) ** -0.5
MLSTM_HEADS = 4
MLSTM_DK = 256
MLSTM_DV = 512
FOX_HEADS = 32
FOX_KV_HEADS = 2
FOX_GROUP = FOX_HEADS // FOX_KV_HEADS
FOX_HEAD_DIM = 128
FOX_SCALE = FOX_HEAD_DIM ** -0.5
N_GROUPS = 4
EXPERTS_PER_GROUP = 8
N_EXPERTS = N_GROUPS * EXPERTS_PER_GROUP
TOP_K = 2
LN_EPS = 1e-5
RMS_EPS = 1e-6
GATE_PAD = -1e30

LANE = 128
VMEM_LIMIT_BYTES = 56 * 1024 * 1024

ROW_TILE = 1088
COL_TILE = 512
ATTN_TILE = 512
MLSTM_CHUNK = 256
MOE_TILE = 256
DECODE_PAGES = 16


def _tile(n, pref, mult=8):
    if n <= pref:
        return n
    for t in range(pref, 0, -1):
        if n % t == 0 and t % mult == 0:
            return t
    return n


def _params(*sem):
    return pltpu.CompilerParams(dimension_semantics=sem, vmem_limit_bytes=VMEM_LIMIT_BYTES)


def _dot(a, b):
    return jnp.dot(a, b, preferred_element_type=F32)


def _dot_nt(a, b):
    return lax.dot_general(a, b, (((1,), (1,)), ((), ())), preferred_element_type=F32)


def _dot_tn(a, b):
    return lax.dot_general(a, b, (((0,), (0,)), ((), ())), preferred_element_type=F32)


def _split3(x):
    hi = x.astype(BF16)
    r1 = x - hi.astype(F32)
    mid = r1.astype(BF16)
    lo = (r1 - mid.astype(F32)).astype(BF16)
    return hi, mid, lo


def _log_sigmoid(x):
    return jnp.minimum(x, 0.0) - jnp.log(1.0 + jnp.exp(-jnp.abs(x)))


def _sigmoid(x):
    return 1.0 / (1.0 + jnp.exp(-x))


def _swap_halves(x):
    lane = lax.broadcasted_iota(jnp.int32, x.shape, 1)
    return jnp.where((lane % ROPE_DIM) < ROPE_DIM // 2, pltpu.roll(x, LANE - ROPE_DIM // 2, 1), pltpu.roll(x, ROPE_DIM // 2, 1))


def _rope(x, cos, sin):
    return x * cos + _swap_halves(x) * sin


def _mm_kernel(x_ref, w_ref, o_ref):
    o_ref[...] = _dot(x_ref[...].astype(BF16), w_ref[...].astype(BF16)).astype(o_ref.dtype)


def matmul(x, w, *, layer=0, rows=None, xcol=0, k=None, wrow=0, n_out=None, tm=ROW_TILE, tn=COL_TILE, out_dtype=F32):
    m = rows or x.shape[0]
    k = k or x.shape[1]
    n = n_out or w.shape[-1]
    tm, tn = _tile(m, tm, 16), _tile(n, tn, LANE)
    return pl.pallas_call(
        _mm_kernel,
        grid=(m // tm, n // tn),
        in_specs=[pl.BlockSpec((tm, k), lambda i, j: (i, xcol)),
                  pl.BlockSpec((None, k, tn), lambda i, j: (layer, wrow, j))],
        out_specs=pl.BlockSpec((tm, tn), lambda i, j: (i, j)),
        out_shape=jax.ShapeDtypeStruct((m, n), out_dtype),
        compiler_params=_params("parallel", "parallel"),
    )(x, w)


def _mm2_kernel(x1_ref, w1_ref, x2_ref, w2_ref, o_ref):
    o_ref[...] = (_dot(x1_ref[...].astype(BF16), w1_ref[...].astype(BF16))
                  + _dot(x2_ref[...].astype(BF16), w2_ref[...].astype(BF16)))


def matmul_pair(x1, x2, w, *, layer=0, tm=ROW_TILE, tn=COL_TILE):
    m, k = x1.shape
    n = w.shape[-1]
    tm, tn = _tile(m, tm, 16), _tile(n, tn, LANE)
    return pl.pallas_call(
        _mm2_kernel,
        grid=(m // tm, n // tn),
        in_specs=[pl.BlockSpec((tm, k), lambda i, j: (i, 0)),
                  pl.BlockSpec((None, k, tn), lambda i, j: (layer, 0, j)),
                  pl.BlockSpec((tm, k), lambda i, j: (i, 0)),
                  pl.BlockSpec((None, k, tn), lambda i, j: (layer, 1, j))],
        out_specs=pl.BlockSpec((tm, tn), lambda i, j: (i, j)),
        out_shape=jax.ShapeDtypeStruct((m, n), F32),
        compiler_params=_params("parallel", "parallel"),
    )(x1, w, x2, w)


def _qup_kernel(x_ref, g_ref, w_ref, cos_ref, sin_ref, o_ref, *, heads_per_tile):
    x = x_ref[...]
    xn = x * lax.rsqrt(jnp.mean(x * x, axis=-1, keepdims=True) + RMS_EPS) * g_ref[...]
    acc = _dot(xn.astype(BF16), w_ref[...].astype(BF16))
    cos, sin = cos_ref[...], sin_ref[...]
    pieces = []
    for h in range(heads_per_tile):
        pieces.append(acc[:, h * 2 * LANE:h * 2 * LANE + LANE])
        pieces.append(_rope(acc[:, h * 2 * LANE + LANE:(h + 1) * 2 * LANE], cos, sin))
    o_ref[...] = jnp.concatenate(pieces, axis=1)


def q_up_project(z, g, w, cos, sin, *, tm=ROW_TILE, tn=COL_TILE):
    m = z.shape[0]
    n = w.shape[-1]
    tm, tn = _tile(m, tm, 8), _tile(n, tn, 2 * LANE)
    return pl.pallas_call(
        functools.partial(_qup_kernel, heads_per_tile=tn // (2 * LANE)),
        grid=(m // tm, n // tn),
        in_specs=[pl.BlockSpec((tm, Q_LORA), lambda i, j: (i, 0)),
                  pl.BlockSpec((1, Q_LORA), lambda i, j: (0, 0)),
                  pl.BlockSpec((Q_LORA, tn), lambda i, j: (0, j)),
                  pl.BlockSpec((tm, LANE), lambda i, j: (i, 0)),
                  pl.BlockSpec((tm, LANE), lambda i, j: (i, 0))],
        out_specs=pl.BlockSpec((tm, tn), lambda i, j: (i, j)),
        out_shape=jax.ShapeDtypeStruct((m, n), F32),
        compiler_params=_params("parallel", "parallel"),
    )(z, g, w, cos, sin)


def _ab_post_kernel(zkv_ref, g_ref, zs_ref, cos_ref, sin_ref, gb_ref, gm_ref, ckv_ref, kr_ref, gate_ref):
    x = zkv_ref[...]
    ckv_ref[...] = x * lax.rsqrt(jnp.mean(x * x, axis=-1, keepdims=True) + RMS_EPS) * g_ref[...]
    zs = zs_ref[...]
    kr_ref[...] = _rope(zs[:, :LANE], cos_ref[...], sin_ref[...])
    gx = zs[:, LANE:] + gb_ref[...]
    gate_ref[...] = jnp.where(gm_ref[...] > 0.0, _log_sigmoid(gx), gx)


def ab_post(zqkv, zs, g, cos, sin, gate_bias, gate_mask, *, tm=256):
    m = zqkv.shape[0]
    tm = _tile(m, tm, 8)
    row = lambda i: (i, 0)
    fixed = lambda i: (0, 0)
    return pl.pallas_call(
        _ab_post_kernel,
        grid=(m // tm,),
        in_specs=[pl.BlockSpec((tm, KV_LORA), lambda i: (i, Q_LORA // KV_LORA)),
                  pl.BlockSpec((1, KV_LORA), fixed),
                  pl.BlockSpec((tm, 2 * LANE), row),
                  pl.BlockSpec((tm, LANE), row),
                  pl.BlockSpec((tm, LANE), row),
                  pl.BlockSpec((1, LANE), fixed),
                  pl.BlockSpec((1, LANE), fixed)],
        out_specs=[pl.BlockSpec((tm, KV_LORA), row), pl.BlockSpec((tm, LANE), row), pl.BlockSpec((tm, LANE), row)],
        out_shape=[jax.ShapeDtypeStruct((m, KV_LORA), F32), jax.ShapeDtypeStruct((m, LANE), F32),
                   jax.ShapeDtypeStruct((m, LANE), F32)],
        compiler_params=_params("parallel"),
    )(zqkv, g, zs, cos, sin, gate_bias, gate_mask)


def _gate_kernel(z_ref, b_ref, o_ref):
    o_ref[...] = _log_sigmoid(z_ref[...] + b_ref[...])


def log_sigmoid_gate(z, bias, *, col, tm=512):
    m = z.shape[0]
    tm = _tile(m, tm, 8)
    return pl.pallas_call(
        _gate_kernel,
        grid=(m // tm,),
        in_specs=[pl.BlockSpec((tm, LANE), lambda i: (i, col)), pl.BlockSpec((1, LANE), lambda i: (0, 0))],
        out_specs=pl.BlockSpec((tm, LANE), lambda i: (i, 0)),
        out_shape=jax.ShapeDtypeStruct((m, LANE), F32),
        compiler_params=_params("parallel"),
    )(z, bias)


def _head_mm_kernel(x_ref, w_ref, o_ref):
    o_ref[...] = _dot(x_ref[...].astype(BF16), w_ref[...].astype(BF16))


def head_matmul(x, w, *, xstride=1):
    m = x.shape[0]
    h, k, n = w.shape
    return pl.pallas_call(
        _head_mm_kernel,
        grid=(h,),
        in_specs=[pl.BlockSpec((m, k), lambda i: (0, i * xstride)), pl.BlockSpec((None, k, n), lambda i: (i, 0, 0))],
        out_specs=pl.BlockSpec((m, n), lambda i: (0, i)),
        out_shape=jax.ShapeDtypeStruct((m, h * n), F32),
        compiler_params=_params("parallel"),
    )(x, w)


def _ln_kernel(x_ref, h_ref, g_ref, b_ref, o_ref, obf_ref, *, alpha):
    z = alpha * x_ref[...] + h_ref[...]
    mu = jnp.mean(z, axis=-1, keepdims=True)
    d = z - mu
    var = jnp.mean(d * d, axis=-1, keepdims=True)
    y = d * lax.rsqrt(var + LN_EPS) * g_ref[...] + b_ref[...]
    o_ref[...] = y
    obf_ref[...] = y.astype(BF16)


def deepnorm_ln(x, h, g, b, *, layer, alpha, tm=128):
    m, d = x.shape
    tm = _tile(m, tm, 16)
    row = lambda i: (i, 0)
    par = lambda i: (layer, 0, 0)
    return pl.pallas_call(
        functools.partial(_ln_kernel, alpha=alpha),
        grid=(m // tm,),
        in_specs=[pl.BlockSpec((tm, d), row), pl.BlockSpec((tm, d), row),
                  pl.BlockSpec((None, 1, d), par), pl.BlockSpec((None, 1, d), par)],
        out_specs=[pl.BlockSpec((tm, d), row), pl.BlockSpec((tm, d), row)],
        out_shape=[jax.ShapeDtypeStruct((m, d), F32), jax.ShapeDtypeStruct((m, d), BF16)],
        compiler_params=_params("parallel"),
    )(x, h, g, b)


def _router_kernel(x_ref, w_ref, o_ref):
    xh, xm, _ = _split3(x_ref[...])
    wh, wm, _ = _split3(w_ref[...])
    o_ref[...] = _dot(xh, wh) + (_dot(xh, wm) + _dot(xm, wh))


def router_logits(x, w, *, tm=256):
    m, d = x.shape
    n = w.shape[1]
    tm = _tile(m, tm, 8)
    return pl.pallas_call(
        _router_kernel,
        grid=(m // tm,),
        in_specs=[pl.BlockSpec((tm, d), lambda i: (i, 0)), pl.BlockSpec((d, n), lambda i: (0, 0))],
        out_specs=pl.BlockSpec((tm, n), lambda i: (i, 0)),
        out_shape=jax.ShapeDtypeStruct((m, n), F32),
        compiler_params=_params("parallel"),
    )(x, w)


def _flash_kernel(*refs, scale, has_k2, has_bias, tq, tk):
    refs = list(refs)
    q_ref, k1_ref = refs.pop(0), refs.pop(0)
    k2_ref = refs.pop(0) if has_k2 else None
    v_ref = refs.pop(0)
    bq_ref, bk_ref = (refs.pop(0), refs.pop(0)) if has_bias else (None, None)
    o_ref, m_sc, l_sc, acc_sc = refs
    qi, ki = pl.program_id(2), pl.program_id(3)

    @pl.when(ki == 0)
    def _():
        m_sc[...] = jnp.full(m_sc.shape, NEG_INF, F32)
        l_sc[...] = jnp.zeros(l_sc.shape, F32)
        acc_sc[...] = jnp.zeros(acc_sc.shape, F32)

    @pl.when(ki <= qi)
    def _():
        k = k1_ref[...]
        if has_k2:
            k = jnp.concatenate([k, k2_ref[...]], axis=1)
        s = _dot_nt(q_ref[...].astype(BF16), k.astype(BF16)) * scale
        if has_bias:
            s = s + bq_ref[...] - bk_ref[...]
        row = qi * tq + lax.broadcasted_iota(jnp.int32, (tq, tk), 0)
        col = ki * tk + lax.broadcasted_iota(jnp.int32, (tq, tk), 1)
        s = jnp.where(col <= row, s, NEG_INF)
        m_prev = m_sc[...]
        m_new = jnp.maximum(m_prev, jnp.max(s, axis=-1, keepdims=True))
        a = jnp.exp(m_prev - m_new)
        p = jnp.exp(s - m_new)
        l_sc[...] = a * l_sc[...] + jnp.sum(p, axis=-1, keepdims=True)
        acc_sc[...] = a * acc_sc[...] + _dot(p.astype(BF16), v_ref[...].astype(BF16))
        m_sc[...] = m_new

    @pl.when(ki == qi)
    def _():
        o_ref[...] = (acc_sc[...] / l_sc[...]).astype(o_ref.dtype)


def flash_attention(q, k1, v, *, batch, seq, heads, scale, dq, q_col, k1_col, v_col, k2=None, bias_q=None, bias_k=None,
                    tile=ATTN_TILE):
    t = _tile(seq, tile, 16)
    nb = seq // t
    has_k2, has_bias = k2 is not None, bias_q is not None
    kv_row = lambda b, h, qi, ki: b * nb + jnp.minimum(ki, qi)
    in_specs = [pl.BlockSpec((t, dq), lambda b, h, qi, ki: (b * nb + qi, q_col(h))),
                pl.BlockSpec((t, LANE), lambda b, h, qi, ki: (kv_row(b, h, qi, ki), k1_col(h)))]
    args = [q, k1]
    if has_k2:
        in_specs.append(pl.BlockSpec((t, LANE), lambda b, h, qi, ki: (kv_row(b, h, qi, ki), 0)))
        args.append(k2)
    in_specs.append(pl.BlockSpec((t, LANE), lambda b, h, qi, ki: (kv_row(b, h, qi, ki), v_col(h))))
    args.append(v)
    if has_bias:
        in_specs.append(pl.BlockSpec((None, t, 1), lambda b, h, qi, ki: (b * heads + h, qi, 0)))
        in_specs.append(pl.BlockSpec((None, 1, t), lambda b, h, qi, ki: (b * heads + h, 0, jnp.minimum(ki, qi))))
        args += [bias_q, bias_k]
    return pl.pallas_call(
        functools.partial(_flash_kernel, scale=scale, has_k2=has_k2, has_bias=has_bias, tq=t, tk=t),
        grid=(batch, heads, nb, nb),
        in_specs=in_specs,
        out_specs=pl.BlockSpec((t, LANE), lambda b, h, qi, ki: (b * nb + qi, h)),
        out_shape=jax.ShapeDtypeStruct((batch * seq, heads * LANE), BF16),
        scratch_shapes=[pltpu.VMEM((t, 1), F32), pltpu.VMEM((t, 1), F32), pltpu.VMEM((t, LANE), F32)],
        compiler_params=_params("parallel", "parallel", "parallel", "arbitrary"),
    )(*args)


def _mlstm_kernel(*refs, chunk, has_init):
    refs = list(refs)
    q_ref, k_ref, v_ref, o_ref, gc_ref, gr_ref, hn_ref = [refs.pop(0) for _ in range(7)]
    c0_ref, n0_ref, m0_ref = [refs.pop(0) for _ in range(3)] if has_init else (None, None, None)
    hm_ref, cout_ref, nout_ref, mout_ref, c_sc, n_sc, m_sc = refs
    L = chunk
    h = pl.program_id(0) % MLSTM_HEADS
    ci = pl.program_id(1)

    @pl.when(ci == 0)
    def _():
        if has_init:
            c_sc[...] = c0_ref[...]
            n_sc[...] = n0_ref[...]
            m_sc[...] = m0_ref[...]
        else:
            c_sc[...] = jnp.zeros(c_sc.shape, F32)
            n_sc[...] = jnp.zeros(n_sc.shape, F32)
            m_sc[...] = jnp.zeros(m_sc.shape, F32)

    gc = gc_ref[...]
    lane = lax.broadcasted_iota(jnp.int32, gc.shape, 1)
    i_col = jnp.sum(jnp.where(lane == h, gc, 0.0), axis=-1, keepdims=True)
    f_col = jnp.sum(jnp.where(lane == MLSTM_HEADS + h, gc, 0.0), axis=-1, keepdims=True)
    gr = gr_ref[...]
    sub = lax.broadcasted_iota(jnp.int32, gr.shape, 0)
    i_row = jnp.sum(jnp.where(sub == h, gr, 0.0), axis=0, keepdims=True)
    f_row = jnp.sum(jnp.where(sub == MLSTM_HEADS + h, gr, 0.0), axis=0, keepdims=True)

    t_idx = lax.broadcasted_iota(jnp.int32, (L, L), 0)
    s_idx = lax.broadcasted_iota(jnp.int32, (L, L), 1)
    causal = s_idx <= t_idx
    b_col = jnp.sum(jnp.where(causal, f_row, 0.0), axis=-1, keepdims=True)
    b_row = jnp.sum(jnp.where(t_idx <= s_idx, f_col, 0.0), axis=0, keepdims=True)
    m_prev = m_sc[...]
    log_d = jnp.where(causal, b_col - b_row + i_row, NEG_INF)
    m_inter = b_col + m_prev
    m_t = jnp.maximum(m_inter, jnp.max(log_d, axis=-1, keepdims=True))
    q = q_ref[...]
    k = k_ref[...] * (MLSTM_DK ** -0.5)
    v_bf = v_ref[...].astype(BF16)
    q_bf = q.astype(BF16)
    s = _dot_nt(q_bf, k.astype(BF16)) * jnp.exp(log_d - m_t)
    inter = jnp.exp(m_inter - m_t)
    c = c_sc[...]
    n = n_sc[...]
    num = _dot(s.astype(BF16), v_bf) + inter * _dot(q_bf, c.astype(BF16))
    den = jnp.sum(s, axis=-1, keepdims=True) + inter * jnp.sum(q * n, axis=-1, keepdims=True)
    hh = num / jnp.maximum(jnp.abs(den), jnp.exp(-m_t))
    m_new = m_t[L - 1:L, :]
    b_last = b_col[L - 1:L, :]
    kw = k * jnp.exp(b_last - b_col + i_col - m_new)
    decay = jnp.exp(b_last + m_prev - m_new)
    c_new = decay * c + _dot_tn(kw.astype(BF16), v_bf)
    n_new = decay * n + jnp.sum(kw, axis=0, keepdims=True)
    c_sc[...] = c_new
    n_sc[...] = n_new
    m_sc[...] = m_new

    mu = jnp.mean(hh, axis=-1, keepdims=True)
    d = hh - mu
    var = jnp.mean(d * d, axis=-1, keepdims=True)
    hm_ref[...] = (d * lax.rsqrt(var + LN_EPS) * hn_ref[...] * _sigmoid(o_ref[...])).astype(hm_ref.dtype)

    @pl.when(ci == pl.num_programs(1) - 1)
    def _():
        cout_ref[...] = c_new
        nout_ref[...] = n_new
        mout_ref[...] = m_new


def mlstm(zm, gates_col, gates_row, head_norm, *, batch, seq, chunk, init=None, out_dtype=BF16):
    H, DK, DV = MLSTM_HEADS, MLSTM_DK, MLSTM_DV
    L = min(chunk, seq)
    nc = seq // L
    has_init = init is not None
    in_specs = [pl.BlockSpec((L, DK), lambda bh, c: ((bh // H) * nc + c, bh % H)),
                pl.BlockSpec((L, DK), lambda bh, c: ((bh // H) * nc + c, H + bh % H)),
                pl.BlockSpec((L, DV), lambda bh, c: ((bh // H) * nc + c, (2 * H * DK) // DV + bh % H)),
                pl.BlockSpec((L, DV), lambda bh, c: ((bh // H) * nc + c, (2 * H * DK) // DV + H + bh % H)),
                pl.BlockSpec((L, LANE), lambda bh, c: ((bh // H) * nc + c, 0)),
                pl.BlockSpec((None, 2 * H, L), lambda bh, c: (bh // H, 0, c)),
                pl.BlockSpec((None, 1, DV), lambda bh, c: (bh % H, 0, 0))]
    args = [zm, zm, zm, zm, gates_col, gates_row, head_norm]
    state = lambda bh, c: (bh, 0, 0)
    if has_init:
        in_specs += [pl.BlockSpec((None, DK, DV), state), pl.BlockSpec((None, 1, DK), state), pl.BlockSpec((None, 1, 1), state)]
        args += list(init)
    return pl.pallas_call(
        functools.partial(_mlstm_kernel, chunk=L, has_init=has_init),
        grid=(batch * H, nc),
        in_specs=in_specs,
        out_specs=[pl.BlockSpec((L, DV), lambda bh, c: ((bh // H) * nc + c, bh % H)),
                   pl.BlockSpec((None, DK, DV), state), pl.BlockSpec((None, 1, DK), state), pl.BlockSpec((None, 1, 1), state)],
        out_shape=[jax.ShapeDtypeStruct((batch * seq, H * DV), out_dtype),
                   jax.ShapeDtypeStruct((batch * H, DK, DV), F32),
                   jax.ShapeDtypeStruct((batch * H, 1, DK), F32),
                   jax.ShapeDtypeStruct((batch * H, 1, 1), F32)],
        scratch_shapes=[pltpu.VMEM((DK, DV), F32), pltpu.VMEM((1, DK), F32), pltpu.VMEM((1, 1), F32)],
        compiler_params=_params("parallel", "arbitrary"),
    )(*args)


def _softmax_update(s_list, v_list, m_sc, l_sc, acc_sc):
    m_prev = m_sc[...]
    m_new = m_prev
    for s in s_list:
        m_new = jnp.maximum(m_new, jnp.max(s, axis=-1, keepdims=True))
    a = jnp.exp(m_prev - m_new)
    l_new = a * l_sc[...]
    acc = a * acc_sc[...]
    for s, v in zip(s_list, v_list):
        p = jnp.exp(s - m_new)
        l_new = l_new + jnp.sum(p, axis=-1, keepdims=True)
        acc = acc + _dot(p.astype(BF16), v)
    m_sc[...] = m_new
    l_sc[...] = l_new
    acc_sc[...] = acc


def _mla_decode_kernel(pt_ref, qlat_ref, qfull_ref, cnew_ref, rnew_ref, *refs, pages, dec_seq):
    lat_refs, rope_refs = refs[:pages], refs[pages:2 * pages]
    o_ref, m_sc, l_sc, acc_sc = refs[2 * pages:]
    j = pl.program_id(1)
    ql = qlat_ref[...].astype(BF16)
    qr = qfull_ref[:, NOPE_DIM:NOPE_DIM + ROPE_DIM].astype(BF16)
    rows = ql.shape[0]

    def scores(c_bf, r):
        return (_dot_nt(ql, c_bf) + _dot_nt(qr, r.astype(BF16))) * MLA_SCALE

    @pl.when(j == 0)
    def _():
        m_sc[...] = jnp.full(m_sc.shape, NEG_INF, F32)
        l_sc[...] = jnp.zeros(l_sc.shape, F32)
        acc_sc[...] = jnp.zeros(acc_sc.shape, F32)
        c_bf = cnew_ref[...].astype(BF16)
        s = scores(c_bf, rnew_ref[:, :ROPE_DIM])
        step = lax.broadcasted_iota(jnp.int32, s.shape, 0) // (rows // dec_seq)
        key = lax.broadcasted_iota(jnp.int32, s.shape, 1)
        _softmax_update([jnp.where(key <= step, s, NEG_INF)], [c_bf], m_sc, l_sc, acc_sc)

    s_list, v_list = [], []
    for lat_ref, rope_ref in zip(lat_refs, rope_refs):
        c_bf = lat_ref[...].astype(BF16)
        s_list.append(scores(c_bf, rope_ref[...]))
        v_list.append(c_bf)
    _softmax_update(s_list, v_list, m_sc, l_sc, acc_sc)

    @pl.when(j == pl.num_programs(1) - 1)
    def _():
        o_ref[...] = acc_sc[...] / l_sc[...]


def mla_decode(page_table, qlat, qfull, c_new, r_new, pool_lat, pool_rope, *, layer, pages=DECODE_PAGES):
    b, rows, _ = qlat.shape
    n_pages = page_table.shape[1]
    page = pool_lat.shape[2]
    pages = _tile(n_pages, pages, 1)
    dec_seq = rows // MLA_HEADS

    def page_spec(width, i):
        return pl.BlockSpec((None, None, page, width), lambda bi, j, pt: (layer, pt[bi * n_pages + j * pages + i], 0, 0))

    per_b = lambda bi, j, pt: (bi, 0, 0)
    in_specs = [pl.BlockSpec((None, rows, KV_LORA), per_b), pl.BlockSpec((None, rows, 2 * LANE), per_b),
                pl.BlockSpec((None, page, KV_LORA), per_b), pl.BlockSpec((None, page, LANE), per_b)]
    in_specs += [page_spec(KV_LORA, i) for i in range(pages)] + [page_spec(ROPE_DIM, i) for i in range(pages)]
    return pl.pallas_call(
        functools.partial(_mla_decode_kernel, pages=pages, dec_seq=dec_seq),
        grid_spec=pltpu.PrefetchScalarGridSpec(
            num_scalar_prefetch=1,
            grid=(b, n_pages // pages),
            in_specs=in_specs,
            out_specs=pl.BlockSpec((None, rows, KV_LORA), per_b),
            scratch_shapes=[pltpu.VMEM((rows, 1), F32), pltpu.VMEM((rows, 1), F32), pltpu.VMEM((rows, KV_LORA), F32)]),
        out_shape=jax.ShapeDtypeStruct((b, rows, KV_LORA), F32),
        compiler_params=_params("parallel", "arbitrary"),
    )(page_table.reshape(-1), qlat, qfull, c_new, r_new, *([pool_lat] * pages), *([pool_rope] * pages))


def _fox_decode_kernel(pt_ref, q_ref, fq_ref, knew_ref, vnew_ref, bnew_ref, *refs, pages):
    k_refs, v_refs, f_refs = refs[:pages], refs[pages:2 * pages], refs[2 * pages:3 * pages]
    o_ref, m_sc, l_sc, acc_sc, tot_sc = refs[3 * pages:]
    j = pl.program_id(1)
    D = FOX_HEAD_DIM
    page = k_refs[0].shape[0]
    dec_seq = q_ref.shape[1] // FOX_GROUP
    q_bf = [q_ref[kh].astype(BF16) for kh in range(FOX_KV_HEADS)]
    fq = [fq_ref[kh] for kh in range(FOX_KV_HEADS)]

    @pl.when(j == 0)
    def _():
        m_sc[...] = jnp.full(m_sc.shape, NEG_INF, F32)
        l_sc[...] = jnp.zeros(l_sc.shape, F32)
        acc_sc[...] = jnp.zeros(acc_sc.shape, F32)
        tot_sc[...] = jnp.zeros(tot_sc.shape, F32)
        for kh in range(FOX_KV_HEADS):
            kb = knew_ref[:, kh * D:(kh + 1) * D].astype(BF16)
            vb = vnew_ref[:, kh * D:(kh + 1) * D].astype(BF16)
            s = _dot_nt(q_bf[kh], kb) * FOX_SCALE + bnew_ref[kh]
            _softmax_update([s], [vb], m_sc.at[kh], l_sc.at[kh], acc_sc.at[kh])

    jj = lax.broadcasted_iota(jnp.int32, (page, page), 0)
    tt = lax.broadcasted_iota(jnp.int32, (page, page), 1)
    after = jnp.concatenate([jnp.where(jj > tt, 1.0, 0.0), jnp.ones((page, page), F32)], axis=1).astype(BF16)
    tot = tot_sc[...]
    s_lists = [[] for _ in range(FOX_KV_HEADS)]
    v_lists = [[] for _ in range(FOX_KV_HEADS)]
    for k_ref, v_ref, f_ref in zip(k_refs, v_refs, f_refs):
        fh, fm, fl = _split3(f_ref[...])
        both = _dot_tn(fh, after) + (_dot_tn(fm, after) + _dot_tn(fl, after))
        suffix = both[:, :page] + tot
        tot = tot + both[:, page:]
        for kh in range(FOX_KV_HEADS):
            bias = jnp.concatenate([suffix[kh * FOX_GROUP:(kh + 1) * FOX_GROUP]] * dec_seq, axis=0) + fq[kh]
            kb = k_ref[:, kh * D:(kh + 1) * D].astype(BF16)
            s_lists[kh].append(_dot_nt(q_bf[kh], kb) * FOX_SCALE + bias)
            v_lists[kh].append(v_ref[:, kh * D:(kh + 1) * D].astype(BF16))
    tot_sc[...] = tot
    for kh in range(FOX_KV_HEADS):
        _softmax_update(s_lists[kh], v_lists[kh], m_sc.at[kh], l_sc.at[kh], acc_sc.at[kh])

    @pl.when(j == pl.num_programs(1) - 1)
    def _():
        o_ref[...] = acc_sc[...] / l_sc[...]


def fox_decode(page_table, q, fq, k_new, v_new, bias_new, pool_k, pool_v, pool_f, *, layer, pages=DECODE_PAGES):
    b, kvh, rows, d = q.shape
    n_pages = page_table.shape[1]
    page = pool_k.shape[2]
    pages = _tile(n_pages, pages, 1)

    def page_spec(width, i):
        return pl.BlockSpec((None, None, page, width),
                            lambda bi, j, pt: (layer, pt[bi * n_pages + (n_pages - 1 - (j * pages + i))], 0, 0))

    per_b4 = lambda bi, j, pt: (bi, 0, 0, 0)
    per_b3 = lambda bi, j, pt: (bi, 0, 0)
    in_specs = [pl.BlockSpec((None, kvh, rows, d), per_b4), pl.BlockSpec((None, kvh, rows, 1), per_b4),
                pl.BlockSpec((None, page, kvh * d), per_b3), pl.BlockSpec((None, page, kvh * d), per_b3),
                pl.BlockSpec((None, kvh, rows, page), per_b4)]
    in_specs += ([page_spec(kvh * d, i) for i in range(pages)] + [page_spec(kvh * d, i) for i in range(pages)]
                 + [page_spec(FOX_HEADS, i) for i in range(pages)])
    return pl.pallas_call(
        functools.partial(_fox_decode_kernel, pages=pages),
        grid_spec=pltpu.PrefetchScalarGridSpec(
            num_scalar_prefetch=1,
            grid=(b, n_pages // pages),
            in_specs=in_specs,
            out_specs=pl.BlockSpec((None, kvh, rows, d), per_b4),
            scratch_shapes=[pltpu.VMEM((kvh, rows, 1), F32), pltpu.VMEM((kvh, rows, 1), F32),
                            pltpu.VMEM((kvh, rows, d), F32), pltpu.VMEM((FOX_HEADS, page), F32)]),
        out_shape=jax.ShapeDtypeStruct((b, kvh, rows, d), F32),
        compiler_params=_params("parallel", "arbitrary"),
    )(page_table.reshape(-1), q, fq, k_new, v_new, bias_new,
      *([pool_k] * pages), *([pool_v] * pages), *([pool_f] * pages))


def _moe_up_kernel(te_ref, tv_ref, x_ref, gate_ref, wg_ref, wu_ref, o_ref):
    t = pl.program_id(0)

    @pl.when(tv_ref[t] > 0)
    def _():
        x = x_ref[...]
        hg = _dot(x, wg_ref[...].astype(BF16))
        hu = _dot(x, wu_ref[...].astype(BF16))
        o_ref[...] = (hg * _sigmoid(hg) * hu * gate_ref[...]).astype(o_ref.dtype)

    @pl.when(tv_ref[t] == 0)
    def _():
        o_ref[...] = jnp.zeros(o_ref.shape, o_ref.dtype)


def _moe_down_kernel(te_ref, tv_ref, h_ref, wd_ref, o_ref):
    t = pl.program_id(0)

    @pl.when(tv_ref[t] > 0)
    def _():
        o_ref[...] = _dot(h_ref[...], wd_ref[...].astype(BF16))

    @pl.when(tv_ref[t] == 0)
    def _():
        o_ref[...] = jnp.zeros(o_ref.shape, o_ref.dtype)


def moe_experts(xs, gate, tile_expert, tile_valid, w_gate, w_up, w_down, *, layer, tm):
    rows, d = xs.shape
    f = w_gate.shape[-1]
    n_tiles = rows // tm
    wspec = lambda a, b: pl.BlockSpec((None, None, a, b), lambda t, te, tv: (layer, te[t], 0, 0))
    act = pl.pallas_call(
        _moe_up_kernel,
        grid_spec=pltpu.PrefetchScalarGridSpec(
            num_scalar_prefetch=2, grid=(n_tiles,),
            in_specs=[pl.BlockSpec((tm, d), lambda t, te, tv: (t, 0)), pl.BlockSpec((tm, 1), lambda t, te, tv: (t, 0)),
                      wspec(d, f), wspec(d, f)],
            out_specs=pl.BlockSpec((tm, f), lambda t, te, tv: (t, 0))),
        out_shape=jax.ShapeDtypeStruct((rows, f), BF16),
        compiler_params=_params("arbitrary"),
    )(tile_expert, tile_valid, xs, gate, w_gate, w_up)
    return pl.pallas_call(
        _moe_down_kernel,
        grid_spec=pltpu.PrefetchScalarGridSpec(
            num_scalar_prefetch=2, grid=(n_tiles,),
            in_specs=[pl.BlockSpec((tm, f), lambda t, te, tv: (t, 0)), wspec(f, d)],
            out_specs=pl.BlockSpec((tm, d), lambda t, te, tv: (t, 0))),
        out_shape=jax.ShapeDtypeStruct((rows, d), F32),
        compiler_params=_params("arbitrary"),
    )(tile_expert, tile_valid, act, w_down)


def hier_moe(x, x_bf, w_group, b_group, w_router, b_router, w_gate, w_up, w_down, *, layer):
    n, d = x.shape
    w_r = jnp.concatenate([w_group[layer], jnp.transpose(w_router[layer], (1, 0, 2)).reshape(d, N_EXPERTS)], axis=1)
    w_r = jnp.pad(w_r, ((0, 0), (0, LANE - w_r.shape[1])))
    logits = router_logits(x, w_r)
    g_logits = logits[:, :N_GROUPS] + b_group[layer]
    g_idx = jnp.argmax(g_logits, axis=-1)
    g_w = jax.nn.softmax(g_logits, axis=-1).max(-1)
    e_all = logits[:, N_GROUPS:N_GROUPS + N_EXPERTS].reshape(n, N_GROUPS, EXPERTS_PER_GROUP) + b_router[layer]
    e_logits = jnp.take_along_axis(e_all, g_idx[:, None, None], axis=1)[:, 0]
    top_v, top_i = lax.top_k(e_logits, TOP_K)
    gates = g_w[:, None] * jax.nn.softmax(top_v, axis=-1)
    expert_id = (g_idx[:, None] * EXPERTS_PER_GROUP + top_i).astype(jnp.int32)

    tm = MOE_TILE
    a = n * TOP_K
    n_tiles = -(-a // tm) + N_EXPERTS
    e_flat = expert_id.reshape(a)
    order = jnp.argsort(e_flat, stable=True)
    counts = jnp.zeros((N_EXPERTS,), jnp.int32).at[e_flat].add(1)
    tiles_per = (counts + tm - 1) // tm
    tile_end = jnp.cumsum(tiles_per)
    row_start = (tile_end - tiles_per) * tm
    sorted_start = jnp.cumsum(counts) - counts
    e_sorted = e_flat[order]
    dest = row_start[e_sorted] + (jnp.arange(a, dtype=jnp.int32) - sorted_start[e_sorted])
    src_tok = jnp.zeros((n_tiles * tm,), jnp.int32).at[dest].set((order // TOP_K).astype(jnp.int32))
    src_gate = jnp.zeros((n_tiles * tm,), F32).at[dest].set(gates.reshape(a)[order])
    pos = jnp.zeros((a,), jnp.int32).at[order].set(dest).reshape(n, TOP_K)
    tile_ids = jnp.arange(n_tiles, dtype=jnp.int32)
    tile_expert = jnp.minimum(jnp.searchsorted(tile_end, tile_ids, side="right"), N_EXPERTS - 1).astype(jnp.int32)
    tile_valid = (tile_ids < tile_end[-1]).astype(jnp.int32)
    tile_expert = jnp.where(tile_valid > 0, tile_expert, tile_expert[jnp.maximum(tile_end[-1] - 1, 0)])

    ys = moe_experts(x_bf[src_tok], src_gate[:, None], tile_expert, tile_valid, w_gate, w_up, w_down, layer=layer, tm=tm)
    return ys[pos[:, 0]] + ys[pos[:, 1]]


def _rope_tables(pos):
    half = ROPE_DIM // 2
    inv = ROPE_THETA ** (-jnp.arange(half, dtype=F32) / half)
    ang = pos[:, None] * inv[None, :]
    cos, sin = jnp.cos(ang), jnp.sin(ang)
    return jnp.concatenate([cos, cos, cos, cos], axis=1), jnp.concatenate([-sin, sin, -sin, sin], axis=1)


def _ab_layer(x, x_bf, dims, cache_lat, cache_rope, st_c, st_n, st_m, page_table, w, j):
    B, T, DB, S = dims
    NP, NS = B * T, DB * S
    H, HM = MLA_HEADS, MLSTM_HEADS
    w_in = w["w_in"]
    d_model = w_in.shape[1]
    m_lo = Q_LORA + KV_LORA + ROPE_DIM
    m_w = 2 * HM * MLSTM_DK + 2 * HM * MLSTM_DV
    zqkv = matmul(x_bf, w_in, layer=j, n_out=Q_LORA + KV_LORA)
    w_small = jnp.concatenate([w_in[j][:, Q_LORA + KV_LORA:m_lo], jnp.zeros((d_model, LANE - ROPE_DIM), F32),
                               w_in[j][:, m_lo + m_w:], jnp.zeros((d_model, LANE - 2 * HM), F32)], axis=1)
    zs = matmul(x_bf, w_small[None])
    zm = matmul(x_bf, w_in[j][:, m_lo:m_lo + m_w][None])

    pos = jnp.concatenate([jnp.tile(jnp.arange(T, dtype=F32), B),
                           jnp.tile(jnp.arange(S, dtype=F32) + page_table.shape[1] * cache_lat.shape[2], DB)])
    cos, sin = _rope_tables(pos)
    gate_bias = jnp.pad(jnp.concatenate([w["b_i"][j], w["b_f"][j]]), (0, LANE - 2 * HM))[None]
    gate_mask = jnp.pad(jnp.concatenate([jnp.zeros((HM,), F32), jnp.ones((HM,), F32)]), (0, LANE - 2 * HM))[None]
    c_kv, k_rope, gates = ab_post(zqkv, zs, w["kv_norm"][j][None], cos, sin, gate_bias, gate_mask)

    wq = w["w_q_up"][j].reshape(Q_LORA, H, NOPE_DIM + ROPE_DIM)
    wq = jnp.pad(wq, ((0, 0), (0, 0), (0, 2 * LANE - NOPE_DIM - ROPE_DIM))).reshape(Q_LORA, H * 2 * LANE)
    q = q_up_project(zqkv, w["q_norm"][j][None], wq, cos, sin)

    w_kv = jnp.concatenate([jnp.transpose(w["w_uk"][j], (1, 0, 2)).reshape(KV_LORA, H * NOPE_DIM),
                            jnp.transpose(w["w_uv"][j], (1, 0, 2)).reshape(KV_LORA, H * MLA_V_DIM)], axis=1)
    kv = matmul(c_kv, w_kv[None], rows=NP, tm=1024, tn=1024)
    o_p = flash_attention(q, kv, kv, batch=B, seq=T, heads=H, scale=MLA_SCALE, dq=2 * LANE,
                          q_col=lambda h: h, k1_col=lambda h: h, v_col=lambda h: H + h, k2=k_rope)

    q_s = q[NP:]
    qlat = head_matmul(q_s, jnp.transpose(w["w_uk"][j], (0, 2, 1)), xstride=2)
    page = cache_lat.shape[2]
    pad_new = lambda a: jnp.pad(a.reshape(DB, S, a.shape[-1]), ((0, 0), (0, page - S), (0, 0)))
    o_lat = mla_decode(page_table, qlat.reshape(DB, S * H, KV_LORA), q_s.reshape(DB, S * H, 2 * LANE),
                       pad_new(c_kv[NP:]), pad_new(k_rope[NP:]), cache_lat, cache_rope, layer=j)
    o_s = head_matmul(o_lat.reshape(NS, H * KV_LORA), w["w_uv"][j])
    o_mla = jnp.concatenate([o_p, o_s.astype(BF16)], axis=0)

    hn = w["head_norm"][j][:, None, :]
    g8 = gates[:, :2 * HM]
    hm_p, c_p, n_p, m_p = mlstm(zm, gates, jnp.transpose(g8[:NP].reshape(B, T, 2 * HM), (0, 2, 1)), hn,
                                batch=B, seq=T, chunk=MLSTM_CHUNK)
    SP = 8
    zm_s = jnp.pad(zm[NP:].reshape(DB, S, -1), ((0, 0), (0, SP - S), (0, 0))).reshape(DB * SP, -1)
    g_s = g8[NP:].reshape(DB, S, 2 * HM)
    g_s = jnp.concatenate([g_s, jnp.broadcast_to(jnp.concatenate([jnp.full((HM,), GATE_PAD, F32), jnp.zeros((HM,), F32)]),
                                                 (DB, SP - S, 2 * HM))], axis=1)
    gates_s = jnp.pad(g_s.reshape(DB * SP, 2 * HM), ((0, 0), (0, LANE - 2 * HM)))
    init = (st_c[j].reshape(DB * HM, MLSTM_DK, MLSTM_DV), st_n[j].reshape(DB * HM, 1, MLSTM_DK), st_m[j].reshape(DB * HM, 1, 1))
    hm_s, c_s, n_s, m_s = mlstm(zm_s, gates_s, jnp.transpose(g_s, (0, 2, 1)), hn, batch=DB, seq=SP, chunk=SP, init=init,
                                out_dtype=F32)
    hm = jnp.concatenate([hm_p, hm_s.reshape(DB, SP, -1)[:, :S].reshape(NS, -1).astype(BF16)], axis=0)

    h = matmul_pair(o_mla, hm, w["w_out"], layer=j)
    state = lambda c, n, m, b: (c.reshape(b, HM, MLSTM_DK, MLSTM_DV), n.reshape(b, HM, MLSTM_DK), m.reshape(b, HM))
    st_p = (c_kv[:NP].reshape(B, T, KV_LORA), k_rope[:NP, :ROPE_DIM].reshape(B, T, ROPE_DIM)) + state(c_p, n_p, m_p, B)
    st_s = (c_kv[NP:].reshape(DB, S, KV_LORA), k_rope[NP:, :ROPE_DIM].reshape(DB, S, ROPE_DIM)) + state(c_s, n_s, m_s, DB)
    return h, st_p, st_s


def _fox_layer(x_bf, dims, cache_k, cache_v, cache_f, page_table, w, j):
    B, T, DB, S = dims
    NP, NS = B * T, DB * S
    HQ, KVH, G, D = FOX_HEADS, FOX_KV_HEADS, FOX_GROUP, FOX_HEAD_DIM
    w_in = w["w_in"]
    zq = matmul(x_bf, w_in, layer=j, n_out=HQ * D)
    kvf_w = 2 * KVH * D + HQ
    w_kvf = jnp.pad(w_in[j][:, HQ * D:], ((0, 0), (0, -kvf_w % LANE)))
    zkvf = matmul(x_bf, w_kvf[None], tn=w_kvf.shape[1])
    logf = log_sigmoid_gate(zkvf, jnp.pad(w["b_f"][j], (0, LANE - HQ))[None], col=2 * KVH)[:, :HQ]

    fcum = jnp.cumsum(logf[:NP].reshape(B, T, HQ), axis=1)
    fcum_t = jnp.transpose(fcum, (0, 2, 1)).reshape(B * HQ, T)
    o_p = flash_attention(zq, zkvf, zkvf, batch=B, seq=T, heads=HQ, scale=FOX_SCALE, dq=D,
                          q_col=lambda h: h, k1_col=lambda h: h // G, v_col=lambda h: KVH + h // G,
                          bias_q=fcum_t[:, :, None], bias_k=fcum_t[:, None, :])

    page = cache_k.shape[2]
    n_pool = cache_k.shape[1]
    q_s = jnp.transpose(zq[NP:].reshape(DB, S, KVH, G, D), (0, 2, 1, 3, 4)).reshape(DB, KVH, S * G, D)
    f_new = jnp.cumsum(logf[NP:].reshape(DB, S, KVH, G), axis=1)
    fq = jnp.transpose(f_new, (0, 2, 1, 3)).reshape(DB, KVH, S * G, 1)
    fk = jnp.transpose(f_new, (0, 2, 3, 1))
    bias_new = fq.reshape(DB, KVH, S, G, 1) - fk[:, :, None, :, :]
    causal = jnp.arange(S)[None, :] <= jnp.arange(S)[:, None]
    bias_new = jnp.where(causal[None, None, :, None, :], bias_new, NEG_INF).reshape(DB, KVH, S * G, S)
    bias_new = jnp.pad(bias_new, ((0, 0), (0, 0), (0, 0), (0, page - S)), constant_values=NEG_INF)
    kv_s = zkvf[NP:].reshape(DB, S, -1)
    pad_new = lambda a: jnp.pad(a, ((0, 0), (0, page - S), (0, 0)))
    o_s = fox_decode(page_table, q_s, fq, pad_new(kv_s[:, :, :KVH * D]), pad_new(kv_s[:, :, KVH * D:2 * KVH * D]), bias_new,
                     cache_k.reshape(cache_k.shape[0], n_pool, page, KVH * D),
                     cache_v.reshape(cache_v.shape[0], n_pool, page, KVH * D), cache_f, layer=j)
    o_s = jnp.transpose(o_s.reshape(DB, KVH, S, G, D), (0, 2, 1, 3, 4)).reshape(NS, HQ * D)
    o = jnp.concatenate([o_p, o_s.astype(BF16)], axis=0)
    h = matmul(o, w["w_out"], layer=j)

    def state(lo, hi):
        k = zkvf[lo:hi, :KVH * D]
        v = zkvf[lo:hi, KVH * D:2 * KVH * D]
        return k, v, logf[lo:hi]
    kp, vp, fp = state(0, NP)
    ks, vs, fs = state(NP, NP + NS)
    st_p = (kp.reshape(B, T, KVH, D), vp.reshape(B, T, KVH, D), fp.reshape(B, T, HQ))
    st_s = (ks.reshape(DB, S, KVH, D), vs.reshape(DB, S, KVH, D), fs.reshape(DB, S, HQ))
    return h, st_p, st_s


def kernel(x_prompt, x_sample, cache_mla_latent, cache_mla_rope, state_mlstm_c, state_mlstm_n, state_mlstm_m, cache_fox_k, cache_fox_v, cache_fox_logf, page_table, ab_w_in, ab_q_norm, ab_kv_norm, ab_w_q_up, ab_w_uk, ab_w_uv, ab_b_i, ab_b_f, ab_head_norm, ab_w_out, fox_w_in, fox_b_f, fox_w_out, ln1_g, ln1_b, ln2_g, ln2_b, moe_w_group, moe_b_group, moe_w_router, moe_b_router, moe_w_gate, moe_w_up, moe_w_down):
    B, T, D = x_prompt.shape
    DB, S, _ = x_sample.shape
    NP = B * T
    dims = (B, T, DB, S)
    depth = ln1_g.shape[0]
    alpha = (2.0 * depth) ** 0.25
    x = jnp.concatenate([x_prompt.reshape(NP, D), x_sample.reshape(DB * S, D)], axis=0)
    x_bf = x.astype(BF16)
    ab = {"w_in": ab_w_in, "q_norm": ab_q_norm, "kv_norm": ab_kv_norm, "w_q_up": ab_w_q_up, "w_uk": ab_w_uk,
          "w_uv": ab_w_uv, "b_i": ab_b_i, "b_f": ab_b_f, "head_norm": ab_head_norm, "w_out": ab_w_out}
    fox = {"w_in": fox_w_in, "b_f": fox_b_f, "w_out": fox_w_out}
    ln = [a[:, None, :] for a in (ln1_g, ln1_b, ln2_g, ln2_b)]
    ab_p, ab_s, fx_p, fx_s = [], [], [], []
    for layer in range(depth):
        j = layer // 2
        if layer % 2 == 0:
            h, st_p, st_s = _ab_layer(x, x_bf, dims, cache_mla_latent, cache_mla_rope, state_mlstm_c, state_mlstm_n,
                                      state_mlstm_m, page_table, ab, j)
            ab_p.append(st_p)
            ab_s.append(st_s)
        else:
            h, st_p, st_s = _fox_layer(x_bf, dims, cache_fox_k, cache_fox_v, cache_fox_logf, page_table, fox, j)
            fx_p.append(st_p)
            fx_s.append(st_s)
        x, x_bf = deepnorm_ln(x, h, ln[0], ln[1], layer=layer, alpha=alpha)
        h = hier_moe(x, x_bf, moe_w_group, moe_b_group, moe_w_router, moe_b_router, moe_w_gate, moe_w_up, moe_w_down,
                     layer=layer)
        x, x_bf = deepnorm_ln(x, h, ln[2], ln[3], layer=layer, alpha=alpha)

    def stack(group, idx, like):
        return jnp.stack([st[idx] for st in group]).astype(like.dtype)

    caches_ab = (cache_mla_latent, cache_mla_rope, state_mlstm_c, state_mlstm_n, state_mlstm_m)
    caches_fx = (cache_fox_k, cache_fox_v, cache_fox_logf)
    out = [x[:NP].reshape(B, T, D), x[NP:].reshape(DB, S, D)]
    out += [stack(ab_p, i, c) for i, c in enumerate(caches_ab)]
    out += [stack(fx_p, i, c) for i, c in enumerate(caches_fx)]
    out += [stack(ab_s, i, c) for i, c in enumerate(caches_ab)]
    out += [stack(fx_s, i, c) for i, c in enumerate(caches_fx)]
    return tuple(out)
```

```python
import functools
import math

import jax
import jax.numpy as jnp
from jax import lax
from jax.experimental import pallas as pl
from jax.experimental.pallas import tpu as pltpu

F32 = jnp.float32
BF16 = jnp.bfloat16
NEG_INF = float("-inf")
LOG2E = math.log2(math.e)

MLA_HEADS = 16
NOPE_DIM = 128
ROPE_DIM = 64
MLA_V_DIM = 128
Q_LORA = 1024
KV_LORA = 512
ROPE_THETA = 10000.0
MLA_SCALE = (NOPE_DIM + ROPE_DIM) ** -0.5
MLSTM_HEADS = 4
MLSTM_DK = 256
MLSTM_DV = 512
FOX_HEADS = 32
FOX_KV_HEADS = 2
FOX_GROUP = FOX_HEADS // FOX_KV_HEADS
FOX_HEAD_DIM = 128
FOX_SCALE = FOX_HEAD_DIM ** -0.5
N_GROUPS = 4
EXPERTS_PER_GROUP = 8
N_EXPERTS = N_GROUPS * EXPERTS_PER_GROUP
TOP_K = 2
LN_EPS = 1e-5
RMS_EPS = 1e-6
GATE_PAD = -1e30

LANE = 128
VMEM_LIMIT_BYTES = 56 * 1024 * 1024

ROW_TILE = 1088
COL_TILE = 512
ATTN_TILE = 512
FLASH_ROWS = 512
MLSTM_CHUNK = 256
MOE_TILE = 256
DECODE_PAGES = 16


def _tile(n, pref, mult=8):
    if n <= pref:
        return n
    for t in range(pref, 0, -1):
        if n % t == 0 and t % mult == 0:
            return t
    return n


def _params(*sem):
    return pltpu.CompilerParams(dimension_semantics=sem, vmem_limit_bytes=VMEM_LIMIT_BYTES)


def _dot(a, b):
    return jnp.dot(a, b, preferred_element_type=F32)


def _dot_nt(a, b):
    return lax.dot_general(a, b, (((1,), (1,)), ((), ())), preferred_element_type=F32)


def _dot_tn(a, b):
    return lax.dot_general(a, b, (((0,), (0,)), ((), ())), preferred_element_type=F32)


def _split3(x):
    hi = x.astype(BF16)
    r1 = x - hi.astype(F32)
    mid = r1.astype(BF16)
    lo = (r1 - mid.astype(F32)).astype(BF16)
    return hi, mid, lo


def _log_sigmoid(x):
    return jnp.minimum(x, 0.0) - jnp.log(1.0 + jnp.exp(-jnp.abs(x)))


def _sigmoid(x):
    return 1.0 / (1.0 + jnp.exp(-x))


def _swap_halves(x):
    lane = lax.broadcasted_iota(jnp.int32, x.shape, 1)
    return jnp.where((lane % ROPE_DIM) < ROPE_DIM // 2, pltpu.roll(x, LANE - ROPE_DIM // 2, 1), pltpu.roll(x, ROPE_DIM // 2, 1))


def _rope(x, cos, sin):
    return x * cos + _swap_halves(x) * sin


def _mm_kernel(x_ref, w_ref, o_ref):
    o_ref[...] = _dot(x_ref[...].astype(BF16), w_ref[...].astype(BF16)).astype(o_ref.dtype)


def matmul(x, w, *, layer=0, rows=None, xcol=0, k=None, wrow=0, n_out=None, tm=ROW_TILE, tn=COL_TILE, out_dtype=F32):
    m = rows or x.shape[0]
    k = k or x.shape[1]
    n = n_out or w.shape[-1]
    tm, tn = _tile(m, tm, 16), _tile(n, tn, LANE)
    return pl.pallas_call(
        _mm_kernel,
        name="matmul",
        grid=(m // tm, n // tn),
        in_specs=[pl.BlockSpec((tm, k), lambda i, j: (i, xcol)),
                  pl.BlockSpec((None, k, tn), lambda i, j: (layer, wrow, j))],
        out_specs=pl.BlockSpec((tm, tn), lambda i, j: (i, j)),
        out_shape=jax.ShapeDtypeStruct((m, n), out_dtype),
        compiler_params=_params("parallel", "parallel"),
    )(x, w)


def _mm2_kernel(x1_ref, w1_ref, x2_ref, w2_ref, o_ref):
    o_ref[...] = (_dot(x1_ref[...].astype(BF16), w1_ref[...].astype(BF16))
                  + _dot(x2_ref[...].astype(BF16), w2_ref[...].astype(BF16)))


def matmul_pair(x1, x2, w, *, layer=0, tm=ROW_TILE, tn=COL_TILE):
    m, k = x1.shape
    n = w.shape[-1]
    tm, tn = _tile(m, tm, 16), _tile(n, tn, LANE)
    return pl.pallas_call(
        _mm2_kernel,
        name="matmul_pair",
        grid=(m // tm, n // tn),
        in_specs=[pl.BlockSpec((tm, k), lambda i, j: (i, 0)),
                  pl.BlockSpec((None, k, tn), lambda i, j: (layer, 0, j)),
                  pl.BlockSpec((tm, k), lambda i, j: (i, 0)),
                  pl.BlockSpec((None, k, tn), lambda i, j: (layer, 1, j))],
        out_specs=pl.BlockSpec((tm, tn), lambda i, j: (i, j)),
        out_shape=jax.ShapeDtypeStruct((m, n), F32),
        compiler_params=_params("parallel", "parallel"),
    )(x1, w, x2, w)


def _qup_kernel(x_ref, g_ref, w_ref, cos_ref, sin_ref, o_ref, *, heads_per_tile):
    x = x_ref[...]
    xn = x * lax.rsqrt(jnp.mean(x * x, axis=-1, keepdims=True) + RMS_EPS) * g_ref[...]
    acc = _dot(xn.astype(BF16), w_ref[...].astype(BF16))
    cos, sin = cos_ref[...], sin_ref[...]
    pieces = []
    for h in range(heads_per_tile):
        pieces.append(acc[:, h * 2 * LANE:h * 2 * LANE + LANE])
        pieces.append(_rope(acc[:, h * 2 * LANE + LANE:(h + 1) * 2 * LANE], cos, sin))
    o_ref[...] = jnp.concatenate(pieces, axis=1)


def q_up_project(z, g, w, cos, sin, *, tm=ROW_TILE, tn=COL_TILE):
    m = z.shape[0]
    n = w.shape[-1]
    tm, tn = _tile(m, tm, 8), _tile(n, tn, 2 * LANE)
    return pl.pallas_call(
        functools.partial(_qup_kernel, heads_per_tile=tn // (2 * LANE)),
        name="q_up_project",
        grid=(m // tm, n // tn),
        in_specs=[pl.BlockSpec((tm, Q_LORA), lambda i, j: (i, 0)),
                  pl.BlockSpec((1, Q_LORA), lambda i, j: (0, 0)),
                  pl.BlockSpec((Q_LORA, tn), lambda i, j: (0, j)),
                  pl.BlockSpec((tm, LANE), lambda i, j: (i, 0)),
                  pl.BlockSpec((tm, LANE), lambda i, j: (i, 0))],
        out_specs=pl.BlockSpec((tm, tn), lambda i, j: (i, j)),
        out_shape=jax.ShapeDtypeStruct((m, n), F32),
        compiler_params=_params("parallel", "parallel"),
    )(z, g, w, cos, sin)


def _ab_post_kernel(zkv_ref, g_ref, zs_ref, cos_ref, sin_ref, gb_ref, gm_ref, ckv_ref, kr_ref, gate_ref):
    x = zkv_ref[...]
    ckv_ref[...] = x * lax.rsqrt(jnp.mean(x * x, axis=-1, keepdims=True) + RMS_EPS) * g_ref[...]
    zs = zs_ref[...]
    kr_ref[...] = _rope(zs[:, :LANE], cos_ref[...], sin_ref[...])
    gx = zs[:, LANE:] + gb_ref[...]
    gate_ref[...] = jnp.where(gm_ref[...] > 0.0, _log_sigmoid(gx), gx)


def ab_post(zqkv, zs, g, cos, sin, gate_bias, gate_mask, *, tm=256):
    m = zqkv.shape[0]
    tm = _tile(m, tm, 8)
    row = lambda i: (i, 0)
    fixed = lambda i: (0, 0)
    return pl.pallas_call(
        _ab_post_kernel,
        name="ab_post",
        grid=(m // tm,),
        in_specs=[pl.BlockSpec((tm, KV_LORA), lambda i: (i, Q_LORA // KV_LORA)),
                  pl.BlockSpec((1, KV_LORA), fixed),
                  pl.BlockSpec((tm, 2 * LANE), row),
                  pl.BlockSpec((tm, LANE), row),
                  pl.BlockSpec((tm, LANE), row),
                  pl.BlockSpec((1, LANE), fixed),
                  pl.BlockSpec((1, LANE), fixed)],
        out_specs=[pl.BlockSpec((tm, KV_LORA), row), pl.BlockSpec((tm, LANE), row), pl.BlockSpec((tm, LANE), row)],
        out_shape=[jax.ShapeDtypeStruct((m, KV_LORA), F32), jax.ShapeDtypeStruct((m, LANE), F32),
                   jax.ShapeDtypeStruct((m, LANE), F32)],
        compiler_params=_params("parallel"),
    )(zqkv, g, zs, cos, sin, gate_bias, gate_mask)


def _gate_kernel(z_ref, b_ref, o_ref):
    o_ref[...] = _log_sigmoid(z_ref[...] + b_ref[...])


def log_sigmoid_gate(z, bias, *, col, tm=512):
    m = z.shape[0]
    tm = _tile(m, tm, 8)
    return pl.pallas_call(
        _gate_kernel,
        name="log_sigmoid_gate",
        grid=(m // tm,),
        in_specs=[pl.BlockSpec((tm, LANE), lambda i: (i, col)), pl.BlockSpec((1, LANE), lambda i: (0, 0))],
        out_specs=pl.BlockSpec((tm, LANE), lambda i: (i, 0)),
        out_shape=jax.ShapeDtypeStruct((m, LANE), F32),
        compiler_params=_params("parallel"),
    )(z, bias)


def _head_mm_kernel(x_ref, w_ref, o_ref):
    o_ref[...] = _dot(x_ref[...].astype(BF16), w_ref[...].astype(BF16))


def head_matmul(x, w, *, xstride=1):
    m = x.shape[0]
    h, k, n = w.shape
    return pl.pallas_call(
        _head_mm_kernel,
        name="head_matmul",
        grid=(h,),
        in_specs=[pl.BlockSpec((m, k), lambda i: (0, i * xstride)), pl.BlockSpec((None, k, n), lambda i: (i, 0, 0))],
        out_specs=pl.BlockSpec((m, n), lambda i: (0, i)),
        out_shape=jax.ShapeDtypeStruct((m, h * n), F32),
        compiler_params=_params("parallel"),
    )(x, w)


def _ln_kernel(x_ref, h_ref, g_ref, b_ref, o_ref, obf_ref, *, alpha):
    z = alpha * x_ref[...] + h_ref[...]
    mu = jnp.mean(z, axis=-1, keepdims=True)
    d = z - mu
    var = jnp.mean(d * d, axis=-1, keepdims=True)
    y = d * lax.rsqrt(var + LN_EPS) * g_ref[...] + b_ref[...]
    o_ref[...] = y
    obf_ref[...] = y.astype(BF16)


def deepnorm_ln(x, h, g, b, *, layer, alpha, tm=128):
    m, d = x.shape
    tm = _tile(m, tm, 16)
    row = lambda i: (i, 0)
    par = lambda i: (layer, 0, 0)
    return pl.pallas_call(
        functools.partial(_ln_kernel, alpha=alpha),
        name="deepnorm_ln",
        grid=(m // tm,),
        in_specs=[pl.BlockSpec((tm, d), row), pl.BlockSpec((tm, d), row),
                  pl.BlockSpec((None, 1, d), par), pl.BlockSpec((None, 1, d), par)],
        out_specs=[pl.BlockSpec((tm, d), row), pl.BlockSpec((tm, d), row)],
        out_shape=[jax.ShapeDtypeStruct((m, d), F32), jax.ShapeDtypeStruct((m, d), BF16)],
        compiler_params=_params("parallel"),
    )(x, h, g, b)


def _router_kernel(x_ref, w_ref, o_ref):
    xh, xm, _ = _split3(x_ref[...])
    wh, wm, _ = _split3(w_ref[...])
    o_ref[...] = _dot(xh, wh) + (_dot(xh, wm) + _dot(xm, wh))


def router_logits(x, w, *, tm=256):
    m, d = x.shape
    n = w.shape[1]
    tm = _tile(m, tm, 8)
    return pl.pallas_call(
        _router_kernel,
        name="router_logits",
        grid=(m // tm,),
        in_specs=[pl.BlockSpec((tm, d), lambda i: (i, 0)), pl.BlockSpec((d, n), lambda i: (0, 0))],
        out_specs=pl.BlockSpec((tm, n), lambda i: (i, 0)),
        out_shape=jax.ShapeDtypeStruct((m, n), F32),
        compiler_params=_params("parallel"),
    )(x, w)


def _flash_kernel(*refs, scale, has_k2, has_bias, tq, tk):
    refs = list(refs)
    q_ref, k1_ref = refs.pop(0), refs.pop(0)
    k2_ref = refs.pop(0) if has_k2 else None
    v_ref = refs.pop(0)
    bq_ref, bk_ref = (refs.pop(0), refs.pop(0)) if has_bias else (None, None)
    o_ref, q_sc, m_sc, l_sc, acc_sc = refs
    qi, ki = pl.program_id(2), pl.program_id(3)

    @pl.when(ki == 0)
    def _():
        q_sc[...] = (q_ref[...] * (scale * LOG2E)).astype(BF16)
        m_sc[...] = jnp.full(m_sc.shape, NEG_INF, F32)
        l_sc[...] = jnp.zeros(l_sc.shape, F32)
        acc_sc[...] = jnp.zeros(acc_sc.shape, F32)

    def step(on_diagonal):
        k = k1_ref[...]
        if has_k2:
            k = jnp.concatenate([k, k2_ref[...]], axis=1)
        k = k.astype(BF16)
        v = v_ref[...].astype(BF16)
        for r0 in range(0, tq, FLASH_ROWS):
            nr = min(FLASH_ROWS, tq - r0)
            rows = pl.ds(r0, nr)
            nk = r0 + nr if on_diagonal else tk
            s = _dot_nt(q_sc[rows, :], k[:nk])
            if has_bias:
                s = s + bq_ref[rows, :] - bk_ref[:, :nk]
            if on_diagonal:
                row = r0 + lax.broadcasted_iota(jnp.int32, s.shape, 0)
                col = lax.broadcasted_iota(jnp.int32, s.shape, 1)
                s = jnp.where(col <= row, s, NEG_INF)
            m_prev = m_sc[rows, :]
            m_new = jnp.maximum(m_prev, jnp.max(s, axis=-1, keepdims=True))
            a = jnp.exp2(m_prev - m_new)
            p = jnp.exp2(s - m_new)
            l_sc[rows, :] = a * l_sc[rows, :] + jnp.sum(p, axis=-1, keepdims=True)
            acc_sc[rows, :] = a * acc_sc[rows, :] + _dot(p.astype(BF16), v[:nk])
            m_sc[rows, :] = m_new

    @pl.when(ki < qi)
    def _():
        step(False)

    @pl.when(ki == qi)
    def _():
        step(True)
        o_ref[...] = (acc_sc[...] / l_sc[...]).astype(o_ref.dtype)


def flash_attention(q, k1, v, *, batch, seq, heads, scale, dq, q_col, k1_col, v_col, k2=None, bias_q=None, bias_k=None,
                    tile=ATTN_TILE):
    t = _tile(seq, tile, 16)
    nb = seq // t
    has_k2, has_bias = k2 is not None, bias_q is not None
    kv_row = lambda b, h, qi, ki: b * nb + jnp.minimum(ki, qi)
    in_specs = [pl.BlockSpec((t, dq), lambda b, h, qi, ki: (b * nb + qi, q_col(h))),
                pl.BlockSpec((t, LANE), lambda b, h, qi, ki: (kv_row(b, h, qi, ki), k1_col(h)))]
    args = [q, k1]
    if has_k2:
        in_specs.append(pl.BlockSpec((t, LANE), lambda b, h, qi, ki: (kv_row(b, h, qi, ki), 0)))
        args.append(k2)
    in_specs.append(pl.BlockSpec((t, LANE), lambda b, h, qi, ki: (kv_row(b, h, qi, ki), v_col(h))))
    args.append(v)
    if has_bias:
        in_specs.append(pl.BlockSpec((None, t, 1), lambda b, h, qi, ki: (b * heads + h, qi, 0)))
        in_specs.append(pl.BlockSpec((None, 1, t), lambda b, h, qi, ki: (b * heads + h, 0, jnp.minimum(ki, qi))))
        args += [bias_q, bias_k]
    return pl.pallas_call(
        functools.partial(_flash_kernel, scale=scale, has_k2=has_k2, has_bias=has_bias, tq=t, tk=t),
        name="flash_bias" if has_bias else "flash_mla",
        grid=(batch, heads, nb, nb),
        in_specs=in_specs,
        out_specs=pl.BlockSpec((t, LANE), lambda b, h, qi, ki: (b * nb + qi, h)),
        out_shape=jax.ShapeDtypeStruct((batch * seq, heads * LANE), BF16),
        scratch_shapes=[pltpu.VMEM((t, dq), BF16), pltpu.VMEM((t, 1), F32), pltpu.VMEM((t, 1), F32), pltpu.VMEM((t, LANE), F32)],
        compiler_params=_params("parallel", "parallel", "parallel", "arbitrary"),
    )(*args)


def _mlstm_kernel(*refs, chunk, has_init):
    refs = list(refs)
    q_ref, k_ref, v_ref, o_ref, gc_ref, gr_ref, hn_ref = [refs.pop(0) for _ in range(7)]
    c0_ref, n0_ref, m0_ref = [refs.pop(0) for _ in range(3)] if has_init else (None, None, None)
    hm_ref, cout_ref, nout_ref, mout_ref, c_sc, n_sc, m_sc = refs
    L = chunk
    h = pl.program_id(0) % MLSTM_HEADS
    ci = pl.program_id(1)

    @pl.when(ci == 0)
    def _():
        if has_init:
            c_sc[...] = c0_ref[...]
            n_sc[...] = n0_ref[...]
            m_sc[...] = m0_ref[...]
        else:
            c_sc[...] = jnp.zeros(c_sc.shape, F32)
            n_sc[...] = jnp.zeros(n_sc.shape, F32)
            m_sc[...] = jnp.zeros(m_sc.shape, F32)

    gc = gc_ref[...]
    lane = lax.broadcasted_iota(jnp.int32, gc.shape, 1)
    i_col = jnp.sum(jnp.where(lane == h, gc, 0.0), axis=-1, keepdims=True)
    f_col = jnp.sum(jnp.where(lane == MLSTM_HEADS + h, gc, 0.0), axis=-1, keepdims=True)
    gr = gr_ref[...]
    sub = lax.broadcasted_iota(jnp.int32, gr.shape, 0)
    i_row = jnp.sum(jnp.where(sub == h, gr, 0.0), axis=0, keepdims=True)
    f_row = jnp.sum(jnp.where(sub == MLSTM_HEADS + h, gr, 0.0), axis=0, keepdims=True)

    t_idx = lax.broadcasted_iota(jnp.int32, (L, L), 0)
    s_idx = lax.broadcasted_iota(jnp.int32, (L, L), 1)
    causal = s_idx <= t_idx
    b_col = jnp.sum(jnp.where(causal, f_row, 0.0), axis=-1, keepdims=True)
    b_row = jnp.sum(jnp.where(t_idx <= s_idx, f_col, 0.0), axis=0, keepdims=True)
    m_prev = m_sc[...]
    log_d = jnp.where(causal, b_col - b_row + i_row, NEG_INF)
    m_inter = b_col + m_prev
    m_t = jnp.maximum(m_inter, jnp.max(log_d, axis=-1, keepdims=True))
    q = q_ref[...]
    k = k_ref[...] * (MLSTM_DK ** -0.5)
    v_bf = v_ref[...].astype(BF16)
    q_bf = q.astype(BF16)
    s = _dot_nt(q_bf, k.astype(BF16)) * jnp.exp(log_d - m_t)
    inter = jnp.exp(m_inter - m_t)
    c = c_sc[...]
    n = n_sc[...]
    num = _dot(s.astype(BF16), v_bf) + inter * _dot(q_bf, c.astype(BF16))
    den = jnp.sum(s, axis=-1, keepdims=True) + inter * jnp.sum(q * n, axis=-1, keepdims=True)
    hh = num / jnp.maximum(jnp.abs(den), jnp.exp(-m_t))
    m_new = m_t[L - 1:L, :]
    b_last = b_col[L - 1:L, :]
    kw = k * jnp.exp(b_last - b_col + i_col - m_new)
    decay = jnp.exp(b_last + m_prev - m_new)
    c_new = decay * c + _dot_tn(kw.astype(BF16), v_bf)
    n_new = decay * n + jnp.sum(kw, axis=0, keepdims=True)
    c_sc[...] = c_new
    n_sc[...] = n_new
    m_sc[...] = m_new

    mu = jnp.mean(hh, axis=-1, keepdims=True)
    d = hh - mu
    var = jnp.mean(d * d, axis=-1, keepdims=True)
    hm_ref[...] = (d * lax.rsqrt(var + LN_EPS) * hn_ref[...] * _sigmoid(o_ref[...])).astype(hm_ref.dtype)

    @pl.when(ci == pl.num_programs(1) - 1)
    def _():
        cout_ref[...] = c_new
        nout_ref[...] = n_new
        mout_ref[...] = m_new


def mlstm(zm, gates_col, gates_row, head_norm, *, batch, seq, chunk, init=None, out_dtype=BF16):
    H, DK, DV = MLSTM_HEADS, MLSTM_DK, MLSTM_DV
    L = min(chunk, seq)
    nc = seq // L
    has_init = init is not None
    in_specs = [pl.BlockSpec((L, DK), lambda bh, c: ((bh // H) * nc + c, bh % H)),
                pl.BlockSpec((L, DK), lambda bh, c: ((bh // H) * nc + c, H + bh % H)),
                pl.BlockSpec((L, DV), lambda bh, c: ((bh // H) * nc + c, (2 * H * DK) // DV + bh % H)),
                pl.BlockSpec((L, DV), lambda bh, c: ((bh // H) * nc + c, (2 * H * DK) // DV + H + bh % H)),
                pl.BlockSpec((L, LANE), lambda bh, c: ((bh // H) * nc + c, 0)),
                pl.BlockSpec((None, 2 * H, L), lambda bh, c: (bh // H, 0, c)),
                pl.BlockSpec((None, 1, DV), lambda bh, c: (bh % H, 0, 0))]
    args = [zm, zm, zm, zm, gates_col, gates_row, head_norm]
    state = lambda bh, c: (bh, 0, 0)
    if has_init:
        in_specs += [pl.BlockSpec((None, DK, DV), state), pl.BlockSpec((None, 1, DK), state), pl.BlockSpec((None, 1, 1), state)]
        args += list(init)
    return pl.pallas_call(
        functools.partial(_mlstm_kernel, chunk=L, has_init=has_init),
        name="mlstm_decode" if has_init else "mlstm_prompt",
        grid=(batch * H, nc),
        in_specs=in_specs,
        out_specs=[pl.BlockSpec((L, DV), lambda bh, c: ((bh // H) * nc + c, bh % H)),
                   pl.BlockSpec((None, DK, DV), state), pl.BlockSpec((None, 1, DK), state), pl.BlockSpec((None, 1, 1), state)],
        out_shape=[jax.ShapeDtypeStruct((batch * seq, H * DV), out_dtype),
                   jax.ShapeDtypeStruct((batch * H, DK, DV), F32),
                   jax.ShapeDtypeStruct((batch * H, 1, DK), F32),
                   jax.ShapeDtypeStruct((batch * H, 1, 1), F32)],
        scratch_shapes=[pltpu.VMEM((DK, DV), F32), pltpu.VMEM((1, DK), F32), pltpu.VMEM((1, 1), F32)],
        compiler_params=_params("parallel", "arbitrary"),
    )(*args)


def _softmax_update(s, v, m_sc, l_sc, acc_sc):
    m_prev = m_sc[...]
    m_new = jnp.maximum(m_prev, jnp.max(s, axis=-1, keepdims=True))
    a = jnp.exp2(m_prev - m_new)
    p = jnp.exp2(s - m_new)
    l_sc[...] = a * l_sc[...] + jnp.sum(p, axis=-1, keepdims=True)
    acc_sc[...] = a * acc_sc[...] + _dot(p.astype(BF16), v)
    m_sc[...] = m_new


def _mla_decode_kernel(pt_ref, qlat_ref, qfull_ref, cnew_ref, rnew_ref, *refs, pages, dec_seq):
    lat_refs, rope_refs = refs[:pages], refs[pages:2 * pages]
    o_ref, m_sc, l_sc, acc_sc = refs[2 * pages:]
    j = pl.program_id(1)
    ql = (qlat_ref[...] * (MLA_SCALE * LOG2E)).astype(BF16)
    qr = (qfull_ref[:, NOPE_DIM:NOPE_DIM + ROPE_DIM] * (MLA_SCALE * LOG2E)).astype(BF16)
    rows = ql.shape[0]

    @pl.when(j == 0)
    def _():
        m_sc[...] = jnp.full(m_sc.shape, NEG_INF, F32)
        l_sc[...] = jnp.zeros(l_sc.shape, F32)
        acc_sc[...] = jnp.zeros(acc_sc.shape, F32)
        c_bf = cnew_ref[...].astype(BF16)
        s = _dot_nt(ql, c_bf) + _dot_nt(qr, rnew_ref[:, :ROPE_DIM].astype(BF16))
        step = lax.broadcasted_iota(jnp.int32, s.shape, 0) // (rows // dec_seq)
        key = lax.broadcasted_iota(jnp.int32, s.shape, 1)
        _softmax_update(jnp.where(key <= step, s, NEG_INF), c_bf, m_sc, l_sc, acc_sc)

    c_all = jnp.concatenate([r[...].astype(BF16) for r in lat_refs], axis=0)
    r_all = jnp.concatenate([r[...].astype(BF16) for r in rope_refs], axis=1)
    _softmax_update(_dot_nt(ql, c_all) + _dot(qr, r_all), c_all, m_sc, l_sc, acc_sc)

    @pl.when(j == pl.num_programs(1) - 1)
    def _():
        o_ref[...] = acc_sc[...] / l_sc[...]


def mla_decode(page_table, qlat, qfull, c_new, r_new, pool_lat, pool_rope_t, *, layer, pages=DECODE_PAGES):
    b, rows, _ = qlat.shape
    n_pages = page_table.shape[1]
    page = pool_lat.shape[2]
    pages = _tile(n_pages, pages, 1)
    dec_seq = rows // MLA_HEADS

    def page_spec(shape, i):
        return pl.BlockSpec((None, None) + shape, lambda bi, j, pt: (layer, pt[bi * n_pages + j * pages + i], 0, 0))

    per_b = lambda bi, j, pt: (bi, 0, 0)
    in_specs = [pl.BlockSpec((None, rows, KV_LORA), per_b), pl.BlockSpec((None, rows, 2 * LANE), per_b),
                pl.BlockSpec((None, page, KV_LORA), per_b), pl.BlockSpec((None, page, LANE), per_b)]
    in_specs += [page_spec((page, KV_LORA), i) for i in range(pages)] + [page_spec((ROPE_DIM, page), i) for i in range(pages)]
    return pl.pallas_call(
        functools.partial(_mla_decode_kernel, pages=pages, dec_seq=dec_seq),
        name="mla_decode",
        grid_spec=pltpu.PrefetchScalarGridSpec(
            num_scalar_prefetch=1,
            grid=(b, n_pages // pages),
            in_specs=in_specs,
            out_specs=pl.BlockSpec((None, rows, KV_LORA), per_b),
            scratch_shapes=[pltpu.VMEM((rows, 1), F32), pltpu.VMEM((rows, 1), F32), pltpu.VMEM((rows, KV_LORA), F32)]),
        out_shape=jax.ShapeDtypeStruct((b, rows, KV_LORA), F32),
        compiler_params=_params("parallel", "arbitrary"),
    )(page_table.reshape(-1), qlat, qfull, c_new, r_new, *([pool_lat] * pages), *([pool_rope_t] * pages))


def _fox_decode_kernel(pt_ref, q_ref, fq_ref, knew_ref, vnew_ref, bnew_ref, *refs, pages):
    k_refs, v_refs, f_refs = refs[:pages], refs[pages:2 * pages], refs[2 * pages:3 * pages]
    o_ref, m_sc, l_sc, acc_sc, tot_sc = refs[3 * pages:]
    j = pl.program_id(1)
    D = FOX_HEAD_DIM
    page = f_refs[0].shape[1]
    dec_seq = q_ref.shape[1] // FOX_GROUP
    q_bf = [(q_ref[kh] * (FOX_SCALE * LOG2E)).astype(BF16) for kh in range(FOX_KV_HEADS)]

    @pl.when(j == 0)
    def _():
        m_sc[...] = jnp.full(m_sc.shape, NEG_INF, F32)
        l_sc[...] = jnp.zeros(l_sc.shape, F32)
        acc_sc[...] = jnp.zeros(acc_sc.shape, F32)
        tot_sc[...] = jnp.zeros(tot_sc.shape, F32)
        for kh in range(FOX_KV_HEADS):
            kb = knew_ref[:, kh * D:(kh + 1) * D].astype(BF16)
            vb = vnew_ref[:, kh * D:(kh + 1) * D].astype(BF16)
            _softmax_update(_dot_nt(q_bf[kh], kb) + bnew_ref[kh], vb, m_sc.at[kh], l_sc.at[kh], acc_sc.at[kh])

    jj = lax.broadcasted_iota(jnp.int32, (page, page), 0)
    tt = lax.broadcasted_iota(jnp.int32, (page, page), 1)
    after = jnp.concatenate([jnp.where(jj > tt, 1.0, 0.0), jnp.ones((page, page), F32)], axis=1).astype(BF16)
    tot = tot_sc[...]
    suffix = []
    for f_ref in f_refs:
        fh, fm, fl = _split3(f_ref[...])
        both = _dot(fh, after) + (_dot(fm, after) + _dot(fl, after))
        suffix.append((both[:, :page] + tot) * LOG2E)
        tot = tot + both[:, page:]
    tot_sc[...] = tot
    for kh in range(FOX_KV_HEADS):
        k_all = jnp.concatenate([r[pl.ds(kh, page, stride=FOX_KV_HEADS), :].astype(BF16) for r in k_refs], axis=0)
        v_all = jnp.concatenate([r[pl.ds(kh, page, stride=FOX_KV_HEADS), :].astype(BF16) for r in v_refs], axis=0)
        bias = jnp.concatenate([jnp.concatenate([sf[kh * FOX_GROUP:(kh + 1) * FOX_GROUP]] * dec_seq, axis=0) for sf in suffix],
                               axis=1) + fq_ref[kh]
        _softmax_update(_dot_nt(q_bf[kh], k_all) + bias, v_all, m_sc.at[kh], l_sc.at[kh], acc_sc.at[kh])

    @pl.when(j == pl.num_programs(1) - 1)
    def _():
        o_ref[...] = acc_sc[...] / l_sc[...]


def fox_decode(page_table, q, fq, k_new, v_new, bias_new, pool_k, pool_v, pool_f_t, *, layer, pages=DECODE_PAGES):
    b, kvh, rows, d = q.shape
    n_pages = page_table.shape[1]
    page = pool_f_t.shape[3]
    pages = _tile(n_pages, pages, 1)

    def page_spec(shape, i):
        return pl.BlockSpec((None, None) + shape,
                            lambda bi, j, pt: (layer, pt[bi * n_pages + (n_pages - 1 - (j * pages + i))], 0, 0))

    per_b4 = lambda bi, j, pt: (bi, 0, 0, 0)
    per_b3 = lambda bi, j, pt: (bi, 0, 0)
    in_specs = [pl.BlockSpec((None, kvh, rows, d), per_b4), pl.BlockSpec((None, kvh, rows, 1), per_b4),
                pl.BlockSpec((None, page, kvh * d), per_b3), pl.BlockSpec((None, page, kvh * d), per_b3),
                pl.BlockSpec((None, kvh, rows, page), per_b4)]
    in_specs += ([page_spec((page * kvh, d), i) for i in range(pages)] + [page_spec((page * kvh, d), i) for i in range(pages)]
                 + [page_spec((FOX_HEADS, page), i) for i in range(pages)])
    return pl.pallas_call(
        functools.partial(_fox_decode_kernel, pages=pages),
        name="fox_decode",
        grid_spec=pltpu.PrefetchScalarGridSpec(
            num_scalar_prefetch=1,
            grid=(b, n_pages // pages),
            in_specs=in_specs,
            out_specs=pl.BlockSpec((None, kvh, rows, d), per_b4),
            scratch_shapes=[pltpu.VMEM((kvh, rows, 1), F32), pltpu.VMEM((kvh, rows, 1), F32),
                            pltpu.VMEM((kvh, rows, d), F32), pltpu.VMEM((FOX_HEADS, page), F32)]),
        out_shape=jax.ShapeDtypeStruct((b, kvh, rows, d), F32),
        compiler_params=_params("parallel", "arbitrary"),
    )(page_table.reshape(-1), q, fq, k_new, v_new, bias_new,
      *([pool_k] * pages), *([pool_v] * pages), *([pool_f_t] * pages))


def _moe_up_kernel(te_ref, tv_ref, x_ref, gate_ref, wg_ref, wu_ref, o_ref):
    t = pl.program_id(0)

    @pl.when(tv_ref[t] > 0)
    def _():
        x = x_ref[...].astype(BF16)
        hg = _dot(x, wg_ref[...].astype(BF16))
        hu = _dot(x, wu_ref[...].astype(BF16))
        o_ref[...] = (hg * _sigmoid(hg) * hu * gate_ref[...]).astype(o_ref.dtype)

    @pl.when(tv_ref[t] == 0)
    def _():
        o_ref[...] = jnp.zeros(o_ref.shape, o_ref.dtype)


def _moe_down_kernel(te_ref, tv_ref, h_ref, wd_ref, o_ref):
    t = pl.program_id(0)

    @pl.when(tv_ref[t] > 0)
    def _():
        o_ref[...] = _dot(h_ref[...], wd_ref[...].astype(BF16))

    @pl.when(tv_ref[t] == 0)
    def _():
        o_ref[...] = jnp.zeros(o_ref.shape, o_ref.dtype)


def moe_experts(xs, gate, tile_expert, tile_valid, w_gate, w_up, w_down, *, layer, tm):
    rows, d = xs.shape
    f = w_gate.shape[-1]
    n_tiles = rows // tm
    wspec = lambda a, b: pl.BlockSpec((None, None, a, b), lambda t, te, tv: (layer, te[t], 0, 0))
    act = pl.pallas_call(
        _moe_up_kernel,
        name="moe_up",
        grid_spec=pltpu.PrefetchScalarGridSpec(
            num_scalar_prefetch=2, grid=(n_tiles,),
            in_specs=[pl.BlockSpec((tm, d), lambda t, te, tv: (t, 0)), pl.BlockSpec((tm, 1), lambda t, te, tv: (t, 0)),
                      wspec(d, f), wspec(d, f)],
            out_specs=pl.BlockSpec((tm, f), lambda t, te, tv: (t, 0))),
        out_shape=jax.ShapeDtypeStruct((rows, f), BF16),
        compiler_params=_params("arbitrary"),
    )(tile_expert, tile_valid, xs, gate, w_gate, w_up)
    return pl.pallas_call(
        _moe_down_kernel,
        name="moe_down",
        grid_spec=pltpu.PrefetchScalarGridSpec(
            num_scalar_prefetch=2, grid=(n_tiles,),
            in_specs=[pl.BlockSpec((tm, f), lambda t, te, tv: (t, 0)), wspec(f, d)],
            out_specs=pl.BlockSpec((tm, d), lambda t, te, tv: (t, 0))),
        out_shape=jax.ShapeDtypeStruct((rows, d), F32),
        compiler_params=_params("arbitrary"),
    )(tile_expert, tile_valid, act, w_down)


def hier_moe(x, w_group, b_group, w_router, b_router, w_gate, w_up, w_down, *, layer):
    n, d = x.shape
    w_r = jnp.concatenate([w_group[layer], jnp.transpose(w_router[layer], (1, 0, 2)).reshape(d, N_EXPERTS)], axis=1)
    w_r = jnp.pad(w_r, ((0, 0), (0, LANE - w_r.shape[1])))
    logits = router_logits(x, w_r)
    g_logits = logits[:, :N_GROUPS] + b_group[layer]
    g_idx = jnp.argmax(g_logits, axis=-1)
    g_w = jax.nn.softmax(g_logits, axis=-1).max(-1)
    e_all = logits[:, N_GROUPS:N_GROUPS + N_EXPERTS].reshape(n, N_GROUPS, EXPERTS_PER_GROUP) + b_router[layer]
    e_logits = jnp.take_along_axis(e_all, g_idx[:, None, None], axis=1)[:, 0]
    top_v, top_i = lax.top_k(e_logits, TOP_K)
    gates = g_w[:, None] * jax.nn.softmax(top_v, axis=-1)
    expert_id = (g_idx[:, None] * EXPERTS_PER_GROUP + top_i).astype(jnp.int32)

    tm = MOE_TILE
    a = n * TOP_K
    n_tiles = -(-a // tm) + N_EXPERTS
    e_flat = expert_id.reshape(a)
    order = jnp.argsort(e_flat, stable=True).astype(jnp.int32)
    e_sorted = e_flat[order]
    bounds = jnp.searchsorted(e_sorted, jnp.arange(N_EXPERTS + 1, dtype=jnp.int32), side="left").astype(jnp.int32)
    sorted_start, counts = bounds[:-1], bounds[1:] - bounds[:-1]
    tiles_per = (counts + tm - 1) // tm
    tile_end = jnp.cumsum(tiles_per)
    row_start = (tile_end - tiles_per) * tm
    tile_ids = jnp.arange(n_tiles, dtype=jnp.int32)
    tile_valid = (tile_ids < tile_end[-1]).astype(jnp.int32)
    tile_expert = jnp.minimum(jnp.searchsorted(tile_end, tile_ids, side="right"), N_EXPERTS - 1).astype(jnp.int32)
    tile_expert = jnp.where(tile_valid > 0, tile_expert, tile_expert[jnp.maximum(tile_end[-1] - 1, 0)])
    r = jnp.arange(n_tiles * tm, dtype=jnp.int32)
    e_r = jnp.repeat(tile_expert, tm)
    k_r = r - row_start[e_r]
    row_valid = (k_r < counts[e_r]) & (jnp.repeat(tile_valid, tm) > 0)
    a_r = order[jnp.clip(sorted_start[e_r] + k_r, 0, a - 1)]
    src_tok = jnp.where(row_valid, a_r // TOP_K, 0)
    src_gate = jnp.where(row_valid, gates.reshape(a)[a_r], 0.0)
    dest_sorted = row_start[e_sorted] + (jnp.arange(a, dtype=jnp.int32) - sorted_start[e_sorted])
    pos = dest_sorted[jnp.argsort(order)].reshape(n, TOP_K)

    ys = moe_experts(x[src_tok], src_gate[:, None], tile_expert, tile_valid, w_gate, w_up, w_down, layer=layer, tm=tm)
    return ys[pos[:, 0]] + ys[pos[:, 1]]


def _rope_tables(pos):
    half = ROPE_DIM // 2
    inv = ROPE_THETA ** (-jnp.arange(half, dtype=F32) / half)
    ang = pos[:, None] * inv[None, :]
    cos, sin = jnp.cos(ang), jnp.sin(ang)
    return jnp.concatenate([cos, cos, cos, cos], axis=1), jnp.concatenate([-sin, sin, -sin, sin], axis=1)


def _ab_layer(x, x_bf, dims, cache_lat, cache_rope, st_c, st_n, st_m, page_table, w, j):
    B, T, DB, S = dims
    NP, NS = B * T, DB * S
    H, HM = MLA_HEADS, MLSTM_HEADS
    w_in = w["w_in"]
    d_model = w_in.shape[1]
    m_lo = Q_LORA + KV_LORA + ROPE_DIM
    m_w = 2 * HM * MLSTM_DK + 2 * HM * MLSTM_DV
    zqkv = matmul(x_bf, w_in, layer=j, n_out=Q_LORA + KV_LORA)
    w_small = jnp.concatenate([w_in[j][:, Q_LORA + KV_LORA:m_lo], jnp.zeros((d_model, LANE - ROPE_DIM), F32),
                               w_in[j][:, m_lo + m_w:], jnp.zeros((d_model, LANE - 2 * HM), F32)], axis=1)
    zs = matmul(x_bf, w_small[None])
    zm = matmul(x_bf, w_in[j][:, m_lo:m_lo + m_w][None])

    pos = jnp.concatenate([jnp.tile(jnp.arange(T, dtype=F32), B),
                           jnp.tile(jnp.arange(S, dtype=F32) + page_table.shape[1] * cache_lat.shape[2], DB)])
    cos, sin = _rope_tables(pos)
    gate_bias = jnp.pad(jnp.concatenate([w["b_i"][j], w["b_f"][j]]), (0, LANE - 2 * HM))[None]
    gate_mask = jnp.pad(jnp.concatenate([jnp.zeros((HM,), F32), jnp.ones((HM,), F32)]), (0, LANE - 2 * HM))[None]
    c_kv, k_rope, gates = ab_post(zqkv, zs, w["kv_norm"][j][None], cos, sin, gate_bias, gate_mask)

    wq = w["w_q_up"][j].reshape(Q_LORA, H, NOPE_DIM + ROPE_DIM)
    wq = jnp.pad(wq, ((0, 0), (0, 0), (0, 2 * LANE - NOPE_DIM - ROPE_DIM))).reshape(Q_LORA, H * 2 * LANE)
    q = q_up_project(zqkv, w["q_norm"][j][None], wq, cos, sin)

    w_kv = jnp.concatenate([jnp.transpose(w["w_uk"][j], (1, 0, 2)).reshape(KV_LORA, H * NOPE_DIM),
                            jnp.transpose(w["w_uv"][j], (1, 0, 2)).reshape(KV_LORA, H * MLA_V_DIM)], axis=1)
    kv = matmul(c_kv, w_kv[None], rows=NP, tm=1024, tn=1024)
    o_p = flash_attention(q, kv, kv, batch=B, seq=T, heads=H, scale=MLA_SCALE, dq=2 * LANE,
                          q_col=lambda h: h, k1_col=lambda h: h, v_col=lambda h: H + h, k2=k_rope)

    q_s = q[NP:]
    qlat = head_matmul(q_s, jnp.transpose(w["w_uk"][j], (0, 2, 1)), xstride=2)
    page = cache_lat.shape[2]
    pad_new = lambda a: jnp.pad(a.reshape(DB, S, a.shape[-1]), ((0, 0), (0, page - S), (0, 0)))
    o_lat = mla_decode(page_table, qlat.reshape(DB, S * H, KV_LORA), q_s.reshape(DB, S * H, 2 * LANE),
                       pad_new(c_kv[NP:]), pad_new(k_rope[NP:]), cache_lat, jnp.swapaxes(cache_rope, 2, 3), layer=j)
    o_s = head_matmul(o_lat.reshape(NS, H * KV_LORA), w["w_uv"][j])
    o_mla = jnp.concatenate([o_p, o_s.astype(BF16)], axis=0)

    hn = w["head_norm"][j][:, None, :]
    g8 = gates[:, :2 * HM]
    hm_p, c_p, n_p, m_p = mlstm(zm, gates, jnp.transpose(g8[:NP].reshape(B, T, 2 * HM), (0, 2, 1)), hn,
                                batch=B, seq=T, chunk=MLSTM_CHUNK)
    SP = 8
    zm_s = jnp.pad(zm[NP:].reshape(DB, S, -1), ((0, 0), (0, SP - S), (0, 0))).reshape(DB * SP, -1)
    g_s = g8[NP:].reshape(DB, S, 2 * HM)
    g_s = jnp.concatenate([g_s, jnp.broadcast_to(jnp.concatenate([jnp.full((HM,), GATE_PAD, F32), jnp.zeros((HM,), F32)]),
                                                 (DB, SP - S, 2 * HM))], axis=1)
    gates_s = jnp.pad(g_s.reshape(DB * SP, 2 * HM), ((0, 0), (0, LANE - 2 * HM)))
    init = (st_c[j].reshape(DB * HM, MLSTM_DK, MLSTM_DV), st_n[j].reshape(DB * HM, 1, MLSTM_DK), st_m[j].reshape(DB * HM, 1, 1))
    hm_s, c_s, n_s, m_s = mlstm(zm_s, gates_s, jnp.transpose(g_s, (0, 2, 1)), hn, batch=DB, seq=SP, chunk=SP, init=init,
                                out_dtype=F32)
    hm = jnp.concatenate([hm_p, hm_s.reshape(DB, SP, -1)[:, :S].reshape(NS, -1).astype(BF16)], axis=0)

    h = matmul_pair(o_mla, hm, w["w_out"], layer=j)
    state = lambda c, n, m, b: (c.reshape(b, HM, MLSTM_DK, MLSTM_DV), n.reshape(b, HM, MLSTM_DK), m.reshape(b, HM))
    st_p = (c_kv[:NP].reshape(B, T, KV_LORA), k_rope[:NP, :ROPE_DIM].reshape(B, T, ROPE_DIM)) + state(c_p, n_p, m_p, B)
    st_s = (c_kv[NP:].reshape(DB, S, KV_LORA), k_rope[NP:, :ROPE_DIM].reshape(DB, S, ROPE_DIM)) + state(c_s, n_s, m_s, DB)
    return h, st_p, st_s


def _fox_layer(x_bf, dims, cache_k, cache_v, cache_f, page_table, w, j):
    B, T, DB, S = dims
    NP, NS = B * T, DB * S
    HQ, KVH, G, D = FOX_HEADS, FOX_KV_HEADS, FOX_GROUP, FOX_HEAD_DIM
    w_in = w["w_in"]
    zq = matmul(x_bf, w_in, layer=j, n_out=HQ * D)
    kvf_w = 2 * KVH * D + HQ
    w_kvf = jnp.pad(w_in[j][:, HQ * D:], ((0, 0), (0, -kvf_w % LANE)))
    zkvf = matmul(x_bf, w_kvf[None], tn=w_kvf.shape[1])
    logf = log_sigmoid_gate(zkvf, jnp.pad(w["b_f"][j], (0, LANE - HQ))[None], col=2 * KVH)[:, :HQ]

    fcum = jnp.cumsum(logf[:NP].reshape(B, T, HQ), axis=1)
    fcum_t = jnp.transpose(fcum, (0, 2, 1)).reshape(B * HQ, T) * LOG2E
    o_p = flash_attention(zq, zkvf, zkvf, batch=B, seq=T, heads=HQ, scale=FOX_SCALE, dq=D,
                          q_col=lambda h: h, k1_col=lambda h: h // G, v_col=lambda h: KVH + h // G,
                          bias_q=fcum_t[:, :, None], bias_k=fcum_t[:, None, :])

    page = cache_k.shape[2]
    n_pool = cache_k.shape[1]
    q_s = jnp.transpose(zq[NP:].reshape(DB, S, KVH, G, D), (0, 2, 1, 3, 4)).reshape(DB, KVH, S * G, D)
    f_new = jnp.cumsum(logf[NP:].reshape(DB, S, KVH, G), axis=1)
    f_new = f_new * LOG2E
    fq = jnp.transpose(f_new, (0, 2, 1, 3)).reshape(DB, KVH, S * G, 1)
    fk = jnp.transpose(f_new, (0, 2, 3, 1))
    bias_new = fq.reshape(DB, KVH, S, G, 1) - fk[:, :, None, :, :]
    causal = jnp.arange(S)[None, :] <= jnp.arange(S)[:, None]
    bias_new = jnp.where(causal[None, None, :, None, :], bias_new, NEG_INF).reshape(DB, KVH, S * G, S)
    bias_new = jnp.pad(bias_new, ((0, 0), (0, 0), (0, 0), (0, page - S)), constant_values=NEG_INF)
    kv_s = zkvf[NP:].reshape(DB, S, -1)
    pad_new = lambda a: jnp.pad(a, ((0, 0), (0, page - S), (0, 0)))
    o_s = fox_decode(page_table, q_s, fq, pad_new(kv_s[:, :, :KVH * D]), pad_new(kv_s[:, :, KVH * D:2 * KVH * D]), bias_new,
                     cache_k.reshape(cache_k.shape[0], n_pool, page * KVH, D),
                     cache_v.reshape(cache_v.shape[0], n_pool, page * KVH, D), jnp.swapaxes(cache_f, 2, 3), layer=j)
    o_s = jnp.transpose(o_s.reshape(DB, KVH, S, G, D), (0, 2, 1, 3, 4)).reshape(NS, HQ * D)
    o = jnp.concatenate([o_p, o_s.astype(BF16)], axis=0)
    h = matmul(o, w["w_out"], layer=j)

    def state(lo, hi):
        k = zkvf[lo:hi, :KVH * D]
        v = zkvf[lo:hi, KVH * D:2 * KVH * D]
        return k, v, logf[lo:hi]
    kp, vp, fp = state(0, NP)
    ks, vs, fs = state(NP, NP + NS)
    st_p = (kp.reshape(B, T, KVH, D), vp.reshape(B, T, KVH, D), fp.reshape(B, T, HQ))
    st_s = (ks.reshape(DB, S, KVH, D), vs.reshape(DB, S, KVH, D), fs.reshape(DB, S, HQ))
    return h, st_p, st_s


def kernel(x_prompt, x_sample, cache_mla_latent, cache_mla_rope, state_mlstm_c, state_mlstm_n, state_mlstm_m, cache_fox_k, cache_fox_v, cache_fox_logf, page_table, ab_w_in, ab_q_norm, ab_kv_norm, ab_w_q_up, ab_w_uk, ab_w_uv, ab_b_i, ab_b_f, ab_head_norm, ab_w_out, fox_w_in, fox_b_f, fox_w_out, ln1_g, ln1_b, ln2_g, ln2_b, moe_w_group, moe_b_group, moe_w_router, moe_b_router, moe_w_gate, moe_w_up, moe_w_down):
    B, T, D = x_prompt.shape
    DB, S, _ = x_sample.shape
    NP = B * T
    dims = (B, T, DB, S)
    depth = ln1_g.shape[0]
    alpha = (2.0 * depth) ** 0.25
    x = jnp.concatenate([x_prompt.reshape(NP, D), x_sample.reshape(DB * S, D)], axis=0)
    x_bf = x.astype(BF16)
    ab = {"w_in": ab_w_in, "q_norm": ab_q_norm, "kv_norm": ab_kv_norm, "w_q_up": ab_w_q_up, "w_uk": ab_w_uk,
          "w_uv": ab_w_uv, "b_i": ab_b_i, "b_f": ab_b_f, "head_norm": ab_head_norm, "w_out": ab_w_out}
    fox = {"w_in": fox_w_in, "b_f": fox_b_f, "w_out": fox_w_out}
    ln = [a[:, None, :] for a in (ln1_g, ln1_b, ln2_g, ln2_b)]
    ab_p, ab_s, fx_p, fx_s = [], [], [], []
    for layer in range(depth):
        j = layer // 2
        if layer % 2 == 0:
            h, st_p, st_s = _ab_layer(x, x_bf, dims, cache_mla_latent, cache_mla_rope, state_mlstm_c, state_mlstm_n,
                                      state_mlstm_m, page_table, ab, j)
            ab_p.append(st_p)
            ab_s.append(st_s)
        else:
            h, st_p, st_s = _fox_layer(x_bf, dims, cache_fox_k, cache_fox_v, cache_fox_logf, page_table, fox, j)
            fx_p.append(st_p)
            fx_s.append(st_s)
        x, x_bf = deepnorm_ln(x, h, ln[0], ln[1], layer=layer, alpha=alpha)
        h = hier_moe(x, moe_w_group, moe_b_group, moe_w_router, moe_b_router, moe_w_gate, moe_w_up, moe_w_down,
                     layer=layer)
        x, x_bf = deepnorm_ln(x, h, ln[2], ln[3], layer=layer, alpha=alpha)

    def stack(group, idx, like):
        return jnp.stack([st[idx] for st in group]).astype(like.dtype)

    caches_ab = (cache_mla_latent, cache_mla_rope, state_mlstm_c, state_mlstm_n, state_mlstm_m)
    caches_fx = (cache_fox_k, cache_fox_v, cache_fox_logf)
    out = [x[:NP].reshape(B, T, D), x[NP:].reshape(DB, S, D)]
    out += [stack(ab_p, i, c) for i, c in enumerate(caches_ab)]
    out += [stack(fx_p, i, c) for i, c in enumerate(caches_fx)]
    out += [stack(ab_s, i, c) for i, c in enumerate(caches_ab)]
    out += [stack(fx_s, i, c) for i, c in enumerate(caches_fx)]
    return tuple(out)
```

```python
import functools
import math

import jax
import jax.numpy as jnp
from jax import lax
from jax.experimental import pallas as pl
from jax.experimental.pallas import tpu as pltpu

F32 = jnp.float32
BF16 = jnp.bfloat16
NEG_INF = float("-inf")
LOG2E = math.log2(math.e)

MLA_HEADS = 16
NOPE_DIM = 128
ROPE_DIM = 64
MLA_V_DIM = 128
Q_LORA = 1024
KV_LORA = 512
ROPE_THETA = 10000.0
MLA_SCALE = (NOPE_DIM + ROPE_DIM) ** -0.5
MLSTM_HEADS = 4
MLSTM_DK = 256
MLSTM_DV = 512
FOX_HEADS = 32
FOX_KV_HEADS = 2
FOX_GROUP = FOX_HEADS // FOX_KV_HEADS
FOX_HEAD_DIM = 128
FOX_SCALE = FOX_HEAD_DIM ** -0.5
N_GROUPS = 4
EXPERTS_PER_GROUP = 8
N_EXPERTS = N_GROUPS * EXPERTS_PER_GROUP
TOP_K = 2
LN_EPS = 1e-5
RMS_EPS = 1e-6
GATE_PAD = -1e30

LANE = 128
VMEM_LIMIT_BYTES = 56 * 1024 * 1024

ROW_TILE = 1088
COL_TILE = 512
ATTN_TILE = 512
MLSTM_CHUNK = 256
MOE_TILE = 256
DECODE_PAGES = 16
DECODE_GROUPS = 2


def _tile(n, pref, mult=8):
    if n <= pref:
        return n
    for t in range(pref, 0, -1):
        if n % t == 0 and t % mult == 0:
            return t
    return n


def _params(*sem):
    return pltpu.CompilerParams(dimension_semantics=sem, vmem_limit_bytes=VMEM_LIMIT_BYTES)


def _dot(a, b):
    return jnp.dot(a, b, preferred_element_type=F32)


def _dot_nt(a, b):
    return lax.dot_general(a, b, (((1,), (1,)), ((), ())), preferred_element_type=F32)


def _dot_tn(a, b):
    return lax.dot_general(a, b, (((0,), (0,)), ((), ())), preferred_element_type=F32)


def _split3(x):
    hi = x.astype(BF16)
    r1 = x - hi.astype(F32)
    mid = r1.astype(BF16)
    lo = (r1 - mid.astype(F32)).astype(BF16)
    return hi, mid, lo


def _log_sigmoid(x):
    return jnp.minimum(x, 0.0) - jnp.log(1.0 + jnp.exp(-jnp.abs(x)))


def _sigmoid(x):
    return 1.0 / (1.0 + jnp.exp(-x))


def _swap_halves(x):
    lane = lax.broadcasted_iota(jnp.int32, x.shape, 1)
    return jnp.where((lane % ROPE_DIM) < ROPE_DIM // 2, pltpu.roll(x, LANE - ROPE_DIM // 2, 1), pltpu.roll(x, ROPE_DIM // 2, 1))


def _rope(x, cos, sin):
    return x * cos + _swap_halves(x) * sin


def _mm_kernel(x_ref, w_ref, o_ref):
    o_ref[...] = _dot(x_ref[...].astype(BF16), w_ref[...].astype(BF16)).astype(o_ref.dtype)


def matmul(x, w, *, layer=0, rows=None, xcol=0, k=None, wrow=0, n_out=None, tm=ROW_TILE, tn=COL_TILE, out_dtype=F32):
    m = rows or x.shape[0]
    k = k or x.shape[1]
    n = n_out or w.shape[-1]
    tm, tn = _tile(m, tm, 16), _tile(n, tn, LANE)
    return pl.pallas_call(
        _mm_kernel,
        name="matmul",
        grid=(m // tm, n // tn),
        in_specs=[pl.BlockSpec((tm, k), lambda i, j: (i, xcol)),
                  pl.BlockSpec((None, k, tn), lambda i, j: (layer, wrow, j))],
        out_specs=pl.BlockSpec((tm, tn), lambda i, j: (i, j)),
        out_shape=jax.ShapeDtypeStruct((m, n), out_dtype),
        compiler_params=_params("parallel", "parallel"),
    )(x, w)


def _mm2_kernel(x1_ref, w1_ref, x2_ref, w2_ref, o_ref):
    o_ref[...] = (_dot(x1_ref[...].astype(BF16), w1_ref[...].astype(BF16))
                  + _dot(x2_ref[...].astype(BF16), w2_ref[...].astype(BF16)))


def matmul_pair(x1, x2, w, *, layer=0, tm=ROW_TILE, tn=COL_TILE):
    m, k = x1.shape
    n = w.shape[-1]
    tm, tn = _tile(m, tm, 16), _tile(n, tn, LANE)
    return pl.pallas_call(
        _mm2_kernel,
        name="matmul_pair",
        grid=(m // tm, n // tn),
        in_specs=[pl.BlockSpec((tm, k), lambda i, j: (i, 0)),
                  pl.BlockSpec((None, k, tn), lambda i, j: (layer, 0, j)),
                  pl.BlockSpec((tm, k), lambda i, j: (i, 0)),
                  pl.BlockSpec((None, k, tn), lambda i, j: (layer, 1, j))],
        out_specs=pl.BlockSpec((tm, tn), lambda i, j: (i, j)),
        out_shape=jax.ShapeDtypeStruct((m, n), F32),
        compiler_params=_params("parallel", "parallel"),
    )(x1, w, x2, w)


def _qup_kernel(x_ref, g_ref, w_ref, cos_ref, sin_ref, o_ref, *, heads_per_tile):
    x = x_ref[...]
    xn = x * lax.rsqrt(jnp.mean(x * x, axis=-1, keepdims=True) + RMS_EPS) * g_ref[...]
    acc = _dot(xn.astype(BF16), w_ref[...].astype(BF16))
    cos, sin = cos_ref[...], sin_ref[...]
    pieces = []
    for h in range(heads_per_tile):
        pieces.append(acc[:, h * 2 * LANE:h * 2 * LANE + LANE])
        pieces.append(_rope(acc[:, h * 2 * LANE + LANE:(h + 1) * 2 * LANE], cos, sin))
    o_ref[...] = jnp.concatenate(pieces, axis=1)


def q_up_project(z, g, w, cos, sin, *, tm=ROW_TILE, tn=COL_TILE):
    m = z.shape[0]
    n = w.shape[-1]
    tm, tn = _tile(m, tm, 8), _tile(n, tn, 2 * LANE)
    return pl.pallas_call(
        functools.partial(_qup_kernel, heads_per_tile=tn // (2 * LANE)),
        name="q_up_project",
        grid=(m // tm, n // tn),
        in_specs=[pl.BlockSpec((tm, Q_LORA), lambda i, j: (i, 0)),
                  pl.BlockSpec((1, Q_LORA), lambda i, j: (0, 0)),
                  pl.BlockSpec((Q_LORA, tn), lambda i, j: (0, j)),
                  pl.BlockSpec((tm, LANE), lambda i, j: (i, 0)),
                  pl.BlockSpec((tm, LANE), lambda i, j: (i, 0))],
        out_specs=pl.BlockSpec((tm, tn), lambda i, j: (i, j)),
        out_shape=jax.ShapeDtypeStruct((m, n), F32),
        compiler_params=_params("parallel", "parallel"),
    )(z, g, w, cos, sin)


def _ab_post_kernel(zkv_ref, g_ref, zs_ref, cos_ref, sin_ref, gb_ref, gm_ref, ckv_ref, kr_ref, gate_ref):
    x = zkv_ref[...]
    ckv_ref[...] = x * lax.rsqrt(jnp.mean(x * x, axis=-1, keepdims=True) + RMS_EPS) * g_ref[...]
    zs = zs_ref[...]
    kr_ref[...] = _rope(zs[:, :LANE], cos_ref[...], sin_ref[...])
    gx = zs[:, LANE:] + gb_ref[...]
    gate_ref[...] = jnp.where(gm_ref[...] > 0.0, _log_sigmoid(gx), gx)


def ab_post(zqkv, zs, g, cos, sin, gate_bias, gate_mask, *, tm=256):
    m = zqkv.shape[0]
    tm = _tile(m, tm, 8)
    row = lambda i: (i, 0)
    fixed = lambda i: (0, 0)
    return pl.pallas_call(
        _ab_post_kernel,
        name="ab_post",
        grid=(m // tm,),
        in_specs=[pl.BlockSpec((tm, KV_LORA), lambda i: (i, Q_LORA // KV_LORA)),
                  pl.BlockSpec((1, KV_LORA), fixed),
                  pl.BlockSpec((tm, 2 * LANE), row),
                  pl.BlockSpec((tm, LANE), row),
                  pl.BlockSpec((tm, LANE), row),
                  pl.BlockSpec((1, LANE), fixed),
                  pl.BlockSpec((1, LANE), fixed)],
        out_specs=[pl.BlockSpec((tm, KV_LORA), row), pl.BlockSpec((tm, LANE), row), pl.BlockSpec((tm, LANE), row)],
        out_shape=[jax.ShapeDtypeStruct((m, KV_LORA), F32), jax.ShapeDtypeStruct((m, LANE), F32),
                   jax.ShapeDtypeStruct((m, LANE), F32)],
        compiler_params=_params("parallel"),
    )(zqkv, g, zs, cos, sin, gate_bias, gate_mask)


def _gate_kernel(z_ref, b_ref, o_ref):
    o_ref[...] = _log_sigmoid(z_ref[...] + b_ref[...])


def log_sigmoid_gate(z, bias, *, col, tm=512):
    m = z.shape[0]
    tm = _tile(m, tm, 8)
    return pl.pallas_call(
        _gate_kernel,
        name="log_sigmoid_gate",
        grid=(m // tm,),
        in_specs=[pl.BlockSpec((tm, LANE), lambda i: (i, col)), pl.BlockSpec((1, LANE), lambda i: (0, 0))],
        out_specs=pl.BlockSpec((tm, LANE), lambda i: (i, 0)),
        out_shape=jax.ShapeDtypeStruct((m, LANE), F32),
        compiler_params=_params("parallel"),
    )(z, bias)


def _head_mm_kernel(x_ref, w_ref, o_ref):
    o_ref[...] = _dot(x_ref[...].astype(BF16), w_ref[...].astype(BF16))


def head_matmul(x, w, *, xstride=1):
    m = x.shape[0]
    h, k, n = w.shape
    return pl.pallas_call(
        _head_mm_kernel,
        name="head_matmul",
        grid=(h,),
        in_specs=[pl.BlockSpec((m, k), lambda i: (0, i * xstride)), pl.BlockSpec((None, k, n), lambda i: (i, 0, 0))],
        out_specs=pl.BlockSpec((m, n), lambda i: (0, i)),
        out_shape=jax.ShapeDtypeStruct((m, h * n), F32),
        compiler_params=_params("parallel"),
    )(x, w)


def _ln_kernel(x_ref, *refs, alpha):
    h_refs, (g_ref, b_ref, o_ref, obf_ref) = refs[:-4], refs[-4:]
    h = h_refs[0][...]
    for h_ref in h_refs[1:]:
        h = h + h_ref[...]
    z = alpha * x_ref[...] + h
    mu = jnp.mean(z, axis=-1, keepdims=True)
    d = z - mu
    var = jnp.mean(d * d, axis=-1, keepdims=True)
    y = d * lax.rsqrt(var + LN_EPS) * g_ref[...] + b_ref[...]
    o_ref[...] = y
    obf_ref[...] = y.astype(BF16)


def deepnorm_ln(x, hs, g, b, *, layer, alpha, tm=128):
    m, d = x.shape
    tm = _tile(m, tm, 16)
    row = lambda i: (i, 0)
    par = lambda i: (layer, 0, 0)
    return pl.pallas_call(
        functools.partial(_ln_kernel, alpha=alpha),
        name="deepnorm_ln",
        grid=(m // tm,),
        in_specs=[pl.BlockSpec((tm, d), row)] * (1 + len(hs)) + [pl.BlockSpec((None, 1, d), par)] * 2,
        out_specs=[pl.BlockSpec((tm, d), row), pl.BlockSpec((tm, d), row)],
        out_shape=[jax.ShapeDtypeStruct((m, d), F32), jax.ShapeDtypeStruct((m, d), BF16)],
        compiler_params=_params("parallel"),
    )(x, *hs, g, b)


def _router_kernel(x_ref, w_ref, o_ref):
    xh, xm, _ = _split3(x_ref[...])
    wh, wm, _ = _split3(w_ref[...])
    o_ref[...] = _dot(xh, wh) + (_dot(xh, wm) + _dot(xm, wh))


def router_logits(x, w, *, tm=256):
    m, d = x.shape
    n = w.shape[1]
    tm = _tile(m, tm, 8)
    return pl.pallas_call(
        _router_kernel,
        name="router_logits",
        grid=(m // tm,),
        in_specs=[pl.BlockSpec((tm, d), lambda i: (i, 0)), pl.BlockSpec((d, n), lambda i: (0, 0))],
        out_specs=pl.BlockSpec((tm, n), lambda i: (i, 0)),
        out_shape=jax.ShapeDtypeStruct((m, n), F32),
        compiler_params=_params("parallel"),
    )(x, w)


def _flash_kernel(*refs, scale, has_k2, has_bias, t, n_kv):
    refs = list(refs)
    q_ref, k1_ref = refs.pop(0), refs.pop(0)
    k2_ref = refs.pop(0) if has_k2 else None
    v_ref = refs.pop(0)
    bq_ref, bk_ref = (refs.pop(0), refs.pop(0)) if has_bias else (None, None)
    (o_ref,) = refs
    q = (q_ref[...] * (scale * LOG2E)).astype(BF16)

    def logits(i):
        keys = pl.ds(i * t, t)
        k = k1_ref[keys, :]
        if has_k2:
            k = jnp.concatenate([k, k2_ref[keys, :]], axis=1)
        s = _dot_nt(q, k.astype(BF16))
        if has_bias:
            s = s + bq_ref[...] - bk_ref[:, keys]
        if i == n_kv - 1:
            row = lax.broadcasted_iota(jnp.int32, (t, t), 0)
            col = lax.broadcasted_iota(jnp.int32, (t, t), 1)
            s = jnp.where(col <= row, s, NEG_INF)
        return s

    m = jnp.full((t, 1), NEG_INF, F32)
    l = jnp.zeros((t, 1), F32)
    acc = jnp.zeros((t, LANE), F32)
    s = logits(0)
    for i in range(n_kv):
        s_next = logits(i + 1) if i + 1 < n_kv else None
        m_new = jnp.maximum(m, jnp.max(s, axis=-1, keepdims=True))
        a = jnp.exp2(m - m_new)
        p = jnp.exp2(s - m_new)
        l = a * l + jnp.sum(p, axis=-1, keepdims=True)
        acc = a * acc + _dot(p.astype(BF16), v_ref[pl.ds(i * t, t), :].astype(BF16))
        m, s = m_new, s_next
    o_ref[...] = (acc / l).astype(o_ref.dtype)


def flash_attention(q, k1, v, *, batch, seq, heads, scale, dq, q_col, k1_col, v_col, k2=None, bias_q=None, bias_k=None,
                    tile=ATTN_TILE):
    t = _tile(seq, tile, 16)
    nb = seq // t
    has_k2, has_bias = k2 is not None, bias_q is not None
    outs = []
    for qi in range(nb):
        in_specs = [pl.BlockSpec((t, dq), lambda b, h, qi=qi: (b * nb + qi, q_col(h))),
                    pl.BlockSpec((seq, LANE), lambda b, h: (b, k1_col(h)))]
        args = [q, k1]
        if has_k2:
            in_specs.append(pl.BlockSpec((seq, LANE), lambda b, h: (b, 0)))
            args.append(k2)
        in_specs.append(pl.BlockSpec((seq, LANE), lambda b, h: (b, v_col(h))))
        args.append(v)
        if has_bias:
            in_specs.append(pl.BlockSpec((None, t, 1), lambda b, h, qi=qi: (b * heads + h, qi, 0)))
            in_specs.append(pl.BlockSpec((None, 1, seq), lambda b, h: (b * heads + h, 0, 0)))
            args += [bias_q, bias_k]
        outs.append(pl.pallas_call(
            functools.partial(_flash_kernel, scale=scale, has_k2=has_k2, has_bias=has_bias, t=t, n_kv=qi + 1),
            name=("flash_bias" if has_bias else "flash_mla") + "_q%d" % qi,
            grid=(batch, heads),
            in_specs=in_specs,
            out_specs=pl.BlockSpec((t, LANE), lambda b, h: (b, h)),
            out_shape=jax.ShapeDtypeStruct((batch * t, heads * LANE), BF16),
            compiler_params=_params("parallel", "parallel"),
        )(*args))
    return jnp.stack([o.reshape(batch, t, heads * LANE) for o in outs], axis=1).reshape(batch * seq, heads * LANE)


def _mlstm_kernel(*refs, chunk, has_init):
    refs = list(refs)
    q_ref, k_ref, v_ref, o_ref, gc_ref, gr_ref, hn_ref = [refs.pop(0) for _ in range(7)]
    c0_ref, n0_ref, m0_ref = [refs.pop(0) for _ in range(3)] if has_init else (None, None, None)
    hm_ref, cout_ref, nout_ref, mout_ref, c_sc, n_sc, m_sc = refs
    L = chunk
    h = pl.program_id(0) % MLSTM_HEADS
    ci = pl.program_id(1)

    @pl.when(ci == 0)
    def _():
        if has_init:
            c_sc[...] = c0_ref[...]
            n_sc[...] = n0_ref[...]
            m_sc[...] = m0_ref[...]
        else:
            c_sc[...] = jnp.zeros(c_sc.shape, F32)
            n_sc[...] = jnp.zeros(n_sc.shape, F32)
            m_sc[...] = jnp.zeros(m_sc.shape, F32)

    gc = gc_ref[...]
    lane = lax.broadcasted_iota(jnp.int32, gc.shape, 1)
    i_col = jnp.sum(jnp.where(lane == h, gc, 0.0), axis=-1, keepdims=True)
    f_col = jnp.sum(jnp.where(lane == MLSTM_HEADS + h, gc, 0.0), axis=-1, keepdims=True)
    gr = gr_ref[...]
    sub = lax.broadcasted_iota(jnp.int32, gr.shape, 0)
    i_row = jnp.sum(jnp.where(sub == h, gr, 0.0), axis=0, keepdims=True)
    f_row = jnp.sum(jnp.where(sub == MLSTM_HEADS + h, gr, 0.0), axis=0, keepdims=True)

    t_idx = lax.broadcasted_iota(jnp.int32, (L, L), 0)
    s_idx = lax.broadcasted_iota(jnp.int32, (L, L), 1)
    causal = s_idx <= t_idx
    b_col = jnp.sum(jnp.where(causal, f_row, 0.0), axis=-1, keepdims=True)
    b_row = jnp.sum(jnp.where(t_idx <= s_idx, f_col, 0.0), axis=0, keepdims=True)
    m_prev = m_sc[...]
    log_d = jnp.where(causal, b_col - b_row + i_row, NEG_INF)
    m_inter = b_col + m_prev
    m_t = jnp.maximum(m_inter, jnp.max(log_d, axis=-1, keepdims=True))
    q = q_ref[...]
    k = k_ref[...] * (MLSTM_DK ** -0.5)
    v_bf = v_ref[...].astype(BF16)
    q_bf = q.astype(BF16)
    s = _dot_nt(q_bf, k.astype(BF16)) * jnp.exp(log_d - m_t)
    inter = jnp.exp(m_inter - m_t)
    c = c_sc[...]
    n = n_sc[...]
    num = _dot(s.astype(BF16), v_bf) + inter * _dot(q_bf, c.astype(BF16))
    den = jnp.sum(s, axis=-1, keepdims=True) + inter * jnp.sum(q * n, axis=-1, keepdims=True)
    hh = num / jnp.maximum(jnp.abs(den), jnp.exp(-m_t))
    m_new = m_t[L - 1:L, :]
    b_last = b_col[L - 1:L, :]
    kw = k * jnp.exp(b_last - b_col + i_col - m_new)
    decay = jnp.exp(b_last + m_prev - m_new)
    c_new = decay * c + _dot_tn(kw.astype(BF16), v_bf)
    n_new = decay * n + jnp.sum(kw, axis=0, keepdims=True)
    c_sc[...] = c_new
    n_sc[...] = n_new
    m_sc[...] = m_new

    mu = jnp.mean(hh, axis=-1, keepdims=True)
    d = hh - mu
    var = jnp.mean(d * d, axis=-1, keepdims=True)
    hm_ref[...] = (d * lax.rsqrt(var + LN_EPS) * hn_ref[...] * _sigmoid(o_ref[...])).astype(hm_ref.dtype)

    @pl.when(ci == pl.num_programs(1) - 1)
    def _():
        cout_ref[...] = c_new
        nout_ref[...] = n_new
        mout_ref[...] = m_new


def mlstm(zm, gates_col, gates_row, head_norm, *, batch, seq, chunk, init=None, out_dtype=BF16):
    H, DK, DV = MLSTM_HEADS, MLSTM_DK, MLSTM_DV
    L = min(chunk, seq)
    nc = seq // L
    has_init = init is not None
    in_specs = [pl.BlockSpec((L, DK), lambda bh, c: ((bh // H) * nc + c, bh % H)),
                pl.BlockSpec((L, DK), lambda bh, c: ((bh // H) * nc + c, H + bh % H)),
                pl.BlockSpec((L, DV), lambda bh, c: ((bh // H) * nc + c, (2 * H * DK) // DV + bh % H)),
                pl.BlockSpec((L, DV), lambda bh, c: ((bh // H) * nc + c, (2 * H * DK) // DV + H + bh % H)),
                pl.BlockSpec((L, LANE), lambda bh, c: ((bh // H) * nc + c, 0)),
                pl.BlockSpec((None, 2 * H, L), lambda bh, c: (bh // H, 0, c)),
                pl.BlockSpec((None, 1, DV), lambda bh, c: (bh % H, 0, 0))]
    args = [zm, zm, zm, zm, gates_col, gates_row, head_norm]
    state = lambda bh, c: (bh, 0, 0)
    if has_init:
        in_specs += [pl.BlockSpec((None, DK, DV), state), pl.BlockSpec((None, 1, DK), state), pl.BlockSpec((None, 1, 1), state)]
        args += list(init)
    return pl.pallas_call(
        functools.partial(_mlstm_kernel, chunk=L, has_init=has_init),
        name="mlstm_decode" if has_init else "mlstm_prompt",
        grid=(batch * H, nc),
        in_specs=in_specs,
        out_specs=[pl.BlockSpec((L, DV), lambda bh, c: ((bh // H) * nc + c, bh % H)),
                   pl.BlockSpec((None, DK, DV), state), pl.BlockSpec((None, 1, DK), state), pl.BlockSpec((None, 1, 1), state)],
        out_shape=[jax.ShapeDtypeStruct((batch * seq, H * DV), out_dtype),
                   jax.ShapeDtypeStruct((batch * H, DK, DV), F32),
                   jax.ShapeDtypeStruct((batch * H, 1, DK), F32),
                   jax.ShapeDtypeStruct((batch * H, 1, 1), F32)],
        scratch_shapes=[pltpu.VMEM((DK, DV), F32), pltpu.VMEM((1, DK), F32), pltpu.VMEM((1, 1), F32)],
        compiler_params=_params("parallel", "arbitrary"),
    )(*args)


def _softmax_update(s, v, m_sc, l_sc, acc_sc):
    m_prev = m_sc[...]
    m_new = jnp.maximum(m_prev, jnp.max(s, axis=-1, keepdims=True))
    a = jnp.exp2(m_prev - m_new)
    p = jnp.exp2(s - m_new)
    l_sc[...] = a * l_sc[...] + jnp.sum(p, axis=-1, keepdims=True)
    acc_sc[...] = a * acc_sc[...] + _dot(p.astype(BF16), v)
    m_sc[...] = m_new


def _softmax_part(s, v):
    m = jnp.max(s, axis=-1, keepdims=True)
    p = jnp.exp2(s - m)
    return m, jnp.sum(p, axis=-1, keepdims=True), _dot(p.astype(BF16), v)


def _softmax_merge(parts, m_sc, l_sc, acc_sc):
    m_prev = m_sc[...]
    m_new = m_prev
    for m, _, _ in parts:
        m_new = jnp.maximum(m_new, m)
    a = jnp.exp2(m_prev - m_new)
    l_new = a * l_sc[...]
    acc = a * acc_sc[...]
    for m, l, pv in parts:
        w = jnp.exp2(m - m_new)
        l_new = l_new + w * l
        acc = acc + w * pv
    m_sc[...] = m_new
    l_sc[...] = l_new
    acc_sc[...] = acc


def _page_groups(pages):
    n = DECODE_GROUPS if pages % DECODE_GROUPS == 0 else 1
    return [range(g * (pages // n), (g + 1) * (pages // n)) for g in range(n)]


def _mla_decode_kernel(pt_ref, qlat_ref, qfull_ref, cnew_ref, rnew_ref, *refs, pages, dec_seq):
    lat_refs, rope_refs = refs[:pages], refs[pages:2 * pages]
    o_ref, m_sc, l_sc, acc_sc = refs[2 * pages:]
    j = pl.program_id(1)
    ql = (qlat_ref[...] * (MLA_SCALE * LOG2E)).astype(BF16)
    qr = (qfull_ref[:, NOPE_DIM:NOPE_DIM + ROPE_DIM] * (MLA_SCALE * LOG2E)).astype(BF16)
    rows = ql.shape[0]

    @pl.when(j == 0)
    def _():
        m_sc[...] = jnp.full(m_sc.shape, NEG_INF, F32)
        l_sc[...] = jnp.zeros(l_sc.shape, F32)
        acc_sc[...] = jnp.zeros(acc_sc.shape, F32)
        c_bf = cnew_ref[...].astype(BF16)
        s = _dot_nt(ql, c_bf) + _dot_nt(qr, rnew_ref[:, :ROPE_DIM].astype(BF16))
        step = lax.broadcasted_iota(jnp.int32, s.shape, 0) // (rows // dec_seq)
        key = lax.broadcasted_iota(jnp.int32, s.shape, 1)
        _softmax_update(jnp.where(key <= step, s, NEG_INF), c_bf, m_sc, l_sc, acc_sc)

    logits, values = [], []
    for g in _page_groups(pages):
        c_g = jnp.concatenate([lat_refs[i][...].astype(BF16) for i in g], axis=0)
        r_g = jnp.concatenate([rope_refs[i][...].astype(BF16) for i in g], axis=1)
        logits.append(_dot_nt(ql, c_g) + _dot(qr, r_g))
        values.append(c_g)
    _softmax_merge([_softmax_part(s, v) for s, v in zip(logits, values)], m_sc, l_sc, acc_sc)

    @pl.when(j == pl.num_programs(1) - 1)
    def _():
        o_ref[...] = acc_sc[...] / l_sc[...]


def mla_decode(page_table, qlat, qfull, c_new, r_new, pool_lat, pool_rope_t, *, layer, pages=DECODE_PAGES):
    b, rows, _ = qlat.shape
    n_pages = page_table.shape[1]
    page = pool_lat.shape[2]
    pages = _tile(n_pages, pages, 1)
    dec_seq = rows // MLA_HEADS

    def page_spec(shape, i):
        return pl.BlockSpec((None, None) + shape, lambda bi, j, pt: (layer, pt[bi * n_pages + j * pages + i], 0, 0))

    per_b = lambda bi, j, pt: (bi, 0, 0)
    in_specs = [pl.BlockSpec((None, rows, KV_LORA), per_b), pl.BlockSpec((None, rows, 2 * LANE), per_b),
                pl.BlockSpec((None, page, KV_LORA), per_b), pl.BlockSpec((None, page, LANE), per_b)]
    in_specs += [page_spec((page, KV_LORA), i) for i in range(pages)] + [page_spec((ROPE_DIM, page), i) for i in range(pages)]
    return pl.pallas_call(
        functools.partial(_mla_decode_kernel, pages=pages, dec_seq=dec_seq),
        name="mla_decode",
        grid_spec=pltpu.PrefetchScalarGridSpec(
            num_scalar_prefetch=1,
            grid=(b, n_pages // pages),
            in_specs=in_specs,
            out_specs=pl.BlockSpec((None, rows, KV_LORA), per_b),
            scratch_shapes=[pltpu.VMEM((rows, 1), F32), pltpu.VMEM((rows, 1), F32), pltpu.VMEM((rows, KV_LORA), F32)]),
        out_shape=jax.ShapeDtypeStruct((b, rows, KV_LORA), F32),
        compiler_params=_params("parallel", "arbitrary"),
    )(page_table.reshape(-1), qlat, qfull, c_new, r_new, *([pool_lat] * pages), *([pool_rope_t] * pages))


def _fox_decode_kernel(pt_ref, q_ref, fq_ref, knew_ref, vnew_ref, bnew_ref, *refs, pages):
    k_refs, v_refs, f_refs = refs[:pages], refs[pages:2 * pages], refs[2 * pages:3 * pages]
    o_ref, m_sc, l_sc, acc_sc, tot_sc = refs[3 * pages:]
    j = pl.program_id(1)
    D = FOX_HEAD_DIM
    page = f_refs[0].shape[1]
    dec_seq = q_ref.shape[1] // FOX_GROUP
    q_bf = [(q_ref[kh] * (FOX_SCALE * LOG2E)).astype(BF16) for kh in range(FOX_KV_HEADS)]

    @pl.when(j == 0)
    def _():
        m_sc[...] = jnp.full(m_sc.shape, NEG_INF, F32)
        l_sc[...] = jnp.zeros(l_sc.shape, F32)
        acc_sc[...] = jnp.zeros(acc_sc.shape, F32)
        tot_sc[...] = jnp.zeros(tot_sc.shape, F32)
        for kh in range(FOX_KV_HEADS):
            kb = knew_ref[:, kh * D:(kh + 1) * D].astype(BF16)
            vb = vnew_ref[:, kh * D:(kh + 1) * D].astype(BF16)
            _softmax_update(_dot_nt(q_bf[kh], kb) + bnew_ref[kh], vb, m_sc.at[kh], l_sc.at[kh], acc_sc.at[kh])

    jj = lax.broadcasted_iota(jnp.int32, (page, page), 0)
    tt = lax.broadcasted_iota(jnp.int32, (page, page), 1)
    after = jnp.concatenate([jnp.where(jj > tt, 1.0, 0.0), jnp.ones((page, page), F32)], axis=1).astype(BF16)
    tot = tot_sc[...]
    suffix = []
    for f_ref in f_refs:
        fh, fm, fl = _split3(f_ref[...])
        both = _dot(fh, after) + (_dot(fm, after) + _dot(fl, after))
        suffix.append((both[:, :page] + tot) * LOG2E)
        tot = tot + both[:, page:]
    tot_sc[...] = tot
    logits, values = [], []
    for kh in range(FOX_KV_HEADS):
        for g in _page_groups(pages):
            k_g = jnp.concatenate([k_refs[i][pl.ds(kh, page, stride=FOX_KV_HEADS), :].astype(BF16) for i in g], axis=0)
            v_g = jnp.concatenate([v_refs[i][pl.ds(kh, page, stride=FOX_KV_HEADS), :].astype(BF16) for i in g], axis=0)
            bias = jnp.concatenate([jnp.concatenate([suffix[i][kh * FOX_GROUP:(kh + 1) * FOX_GROUP]] * dec_seq, axis=0)
                                    for i in g], axis=1) + fq_ref[kh]
            logits.append(_dot_nt(q_bf[kh], k_g) + bias)
            values.append(v_g)
    parts = [_softmax_part(s, v) for s, v in zip(logits, values)]
    n_g = len(parts) // FOX_KV_HEADS
    for kh in range(FOX_KV_HEADS):
        _softmax_merge(parts[kh * n_g:(kh + 1) * n_g], m_sc.at[kh], l_sc.at[kh], acc_sc.at[kh])

    @pl.when(j == pl.num_programs(1) - 1)
    def _():
        o_ref[...] = acc_sc[...] / l_sc[...]


def fox_decode(page_table, q, fq, k_new, v_new, bias_new, pool_k, pool_v, pool_f_t, *, layer, pages=DECODE_PAGES):
    b, kvh, rows, d = q.shape
    n_pages = page_table.shape[1]
    page = pool_f_t.shape[3]
    pages = _tile(n_pages, pages, 1)

    def page_spec(shape, i):
        return pl.BlockSpec((None, None) + shape,
                            lambda bi, j, pt: (layer, pt[bi * n_pages + (n_pages - 1 - (j * pages + i))], 0, 0))

    per_b4 = lambda bi, j, pt: (bi, 0, 0, 0)
    per_b3 = lambda bi, j, pt: (bi, 0, 0)
    in_specs = [pl.BlockSpec((None, kvh, rows, d), per_b4), pl.BlockSpec((None, kvh, rows, 1), per_b4),
                pl.BlockSpec((None, page, kvh * d), per_b3), pl.BlockSpec((None, page, kvh * d), per_b3),
                pl.BlockSpec((None, kvh, rows, page), per_b4)]
    in_specs += ([page_spec((page * kvh, d), i) for i in range(pages)] + [page_spec((page * kvh, d), i) for i in range(pages)]
                 + [page_spec((FOX_HEADS, page), i) for i in range(pages)])
    return pl.pallas_call(
        functools.partial(_fox_decode_kernel, pages=pages),
        name="fox_decode",
        grid_spec=pltpu.PrefetchScalarGridSpec(
            num_scalar_prefetch=1,
            grid=(b, n_pages // pages),
            in_specs=in_specs,
            out_specs=pl.BlockSpec((None, kvh, rows, d), per_b4),
            scratch_shapes=[pltpu.VMEM((kvh, rows, 1), F32), pltpu.VMEM((kvh, rows, 1), F32),
                            pltpu.VMEM((kvh, rows, d), F32), pltpu.VMEM((FOX_HEADS, page), F32)]),
        out_shape=jax.ShapeDtypeStruct((b, kvh, rows, d), F32),
        compiler_params=_params("parallel", "arbitrary"),
    )(page_table.reshape(-1), q, fq, k_new, v_new, bias_new,
      *([pool_k] * pages), *([pool_v] * pages), *([pool_f_t] * pages))


def _moe_up_kernel(te_ref, tv_ref, x_ref, gate_ref, wg_ref, wu_ref, o_ref):
    t = pl.program_id(0)

    @pl.when(tv_ref[t] > 0)
    def _():
        x = x_ref[...]
        hg = _dot(x, wg_ref[...].astype(BF16))
        hu = _dot(x, wu_ref[...].astype(BF16))
        o_ref[...] = (hg * _sigmoid(hg) * hu * gate_ref[...]).astype(o_ref.dtype)

    @pl.when(tv_ref[t] == 0)
    def _():
        o_ref[...] = jnp.zeros(o_ref.shape, o_ref.dtype)


def _moe_down_kernel(te_ref, tv_ref, h_ref, wd_ref, o_ref):
    t = pl.program_id(0)

    @pl.when(tv_ref[t] > 0)
    def _():
        o_ref[...] = _dot(h_ref[...], wd_ref[...].astype(BF16))

    @pl.when(tv_ref[t] == 0)
    def _():
        o_ref[...] = jnp.zeros(o_ref.shape, o_ref.dtype)


def moe_experts(xs, gate, tile_expert, tile_valid, w_gate, w_up, w_down, *, layer, tm):
    rows, d = xs.shape
    f = w_gate.shape[-1]
    n_tiles = rows // tm
    wspec = lambda a, b: pl.BlockSpec((None, None, a, b), lambda t, te, tv: (layer, te[t], 0, 0))
    act = pl.pallas_call(
        _moe_up_kernel,
        name="moe_up",
        grid_spec=pltpu.PrefetchScalarGridSpec(
            num_scalar_prefetch=2, grid=(n_tiles,),
            in_specs=[pl.BlockSpec((tm, d), lambda t, te, tv: (t, 0)), pl.BlockSpec((tm, 1), lambda t, te, tv: (t, 0)),
                      wspec(d, f), wspec(d, f)],
            out_specs=pl.BlockSpec((tm, f), lambda t, te, tv: (t, 0))),
        out_shape=jax.ShapeDtypeStruct((rows, f), BF16),
        compiler_params=_params("arbitrary"),
    )(tile_expert, tile_valid, xs, gate, w_gate, w_up)
    return pl.pallas_call(
        _moe_down_kernel,
        name="moe_down",
        grid_spec=pltpu.PrefetchScalarGridSpec(
            num_scalar_prefetch=2, grid=(n_tiles,),
            in_specs=[pl.BlockSpec((tm, f), lambda t, te, tv: (t, 0)), wspec(f, d)],
            out_specs=pl.BlockSpec((tm, d), lambda t, te, tv: (t, 0))),
        out_shape=jax.ShapeDtypeStruct((rows, d), F32),
        compiler_params=_params("arbitrary"),
    )(tile_expert, tile_valid, act, w_down)


def hier_moe(x, x_bf, w_group, b_group, w_router, b_router, w_gate, w_up, w_down, *, layer):
    n, d = x.shape
    w_r = jnp.concatenate([w_group[layer], jnp.transpose(w_router[layer], (1, 0, 2)).reshape(d, N_EXPERTS)], axis=1)
    w_r = jnp.pad(w_r, ((0, 0), (0, LANE - w_r.shape[1])))
    logits = router_logits(x, w_r)
    g_logits = logits[:, :N_GROUPS] + b_group[layer]
    g_idx = jnp.argmax(g_logits, axis=-1)
    g_w = jax.nn.softmax(g_logits, axis=-1).max(-1)
    e_all = logits[:, N_GROUPS:N_GROUPS + N_EXPERTS].reshape(n, N_GROUPS, EXPERTS_PER_GROUP) + b_router[layer]
    e_logits = jnp.take_along_axis(e_all, g_idx[:, None, None], axis=1)[:, 0]
    top_v, top_i = lax.top_k(e_logits, TOP_K)
    gates = g_w[:, None] * jax.nn.softmax(top_v, axis=-1)
    expert_id = (g_idx[:, None] * EXPERTS_PER_GROUP + top_i).astype(jnp.int32)

    tm = MOE_TILE
    a = n * TOP_K
    n_tiles = -(-a // tm) + N_EXPERTS
    e_flat = expert_id.reshape(a)
    order = jnp.argsort(e_flat, stable=True).astype(jnp.int32)
    e_sorted = e_flat[order]
    bounds = jnp.searchsorted(e_sorted, jnp.arange(N_EXPERTS + 1, dtype=jnp.int32), side="left").astype(jnp.int32)
    sorted_start, counts = bounds[:-1], bounds[1:] - bounds[:-1]
    tiles_per = (counts + tm - 1) // tm
    tile_end = jnp.cumsum(tiles_per)
    row_start = (tile_end - tiles_per) * tm
    tile_ids = jnp.arange(n_tiles, dtype=jnp.int32)
    tile_valid = (tile_ids < tile_end[-1]).astype(jnp.int32)
    tile_expert = jnp.minimum(jnp.searchsorted(tile_end, tile_ids, side="right"), N_EXPERTS - 1).astype(jnp.int32)
    tile_expert = jnp.where(tile_valid > 0, tile_expert, tile_expert[jnp.maximum(tile_end[-1] - 1, 0)])
    r = jnp.arange(n_tiles * tm, dtype=jnp.int32)
    e_r = jnp.repeat(tile_expert, tm)
    k_r = r - row_start[e_r]
    row_valid = (k_r < counts[e_r]) & (jnp.repeat(tile_valid, tm) > 0)
    a_r = order[jnp.clip(sorted_start[e_r] + k_r, 0, a - 1)]
    src_tok = jnp.where(row_valid, a_r // TOP_K, r % n)
    src_gate = jnp.where(row_valid, gates.reshape(a)[a_r], 0.0)
    dest_sorted = row_start[e_sorted] + (jnp.arange(a, dtype=jnp.int32) - sorted_start[e_sorted])
    pos = dest_sorted[jnp.argsort(order)].reshape(n, TOP_K)

    ys = moe_experts(x_bf[src_tok], src_gate[:, None], tile_expert, tile_valid, w_gate, w_up, w_down, layer=layer, tm=tm)
    return ys[pos[:, 0]], ys[pos[:, 1]]


def _rope_tables(pos):
    half = ROPE_DIM // 2
    inv = ROPE_THETA ** (-jnp.arange(half, dtype=F32) / half)
    ang = pos[:, None] * inv[None, :]
    cos, sin = jnp.cos(ang), jnp.sin(ang)
    return jnp.concatenate([cos, cos, cos, cos], axis=1), jnp.concatenate([-sin, sin, -sin, sin], axis=1)


def _ab_layer(x, x_bf, dims, cache_lat, cache_rope, st_c, st_n, st_m, page_table, w, j):
    B, T, DB, S = dims
    NP, NS = B * T, DB * S
    H, HM = MLA_HEADS, MLSTM_HEADS
    w_in = w["w_in"]
    d_model = w_in.shape[1]
    m_lo = Q_LORA + KV_LORA + ROPE_DIM
    m_w = 2 * HM * MLSTM_DK + 2 * HM * MLSTM_DV
    zqkv = matmul(x_bf, w_in, layer=j, n_out=Q_LORA + KV_LORA)
    w_small = jnp.concatenate([w_in[j][:, Q_LORA + KV_LORA:m_lo], jnp.zeros((d_model, LANE - ROPE_DIM), F32),
                               w_in[j][:, m_lo + m_w:], jnp.zeros((d_model, LANE - 2 * HM), F32)], axis=1)
    zs = matmul(x_bf, w_small[None])
    zm = matmul(x_bf, w_in[j][:, m_lo:m_lo + m_w][None])

    pos = jnp.concatenate([jnp.tile(jnp.arange(T, dtype=F32), B),
                           jnp.tile(jnp.arange(S, dtype=F32) + page_table.shape[1] * cache_lat.shape[2], DB)])
    cos, sin = _rope_tables(pos)
    gate_bias = jnp.pad(jnp.concatenate([w["b_i"][j], w["b_f"][j]]), (0, LANE - 2 * HM))[None]
    gate_mask = jnp.pad(jnp.concatenate([jnp.zeros((HM,), F32), jnp.ones((HM,), F32)]), (0, LANE - 2 * HM))[None]
    c_kv, k_rope, gates = ab_post(zqkv, zs, w["kv_norm"][j][None], cos, sin, gate_bias, gate_mask)

    wq = w["w_q_up"][j].reshape(Q_LORA, H, NOPE_DIM + ROPE_DIM)
    wq = jnp.pad(wq, ((0, 0), (0, 0), (0, 2 * LANE - NOPE_DIM - ROPE_DIM))).reshape(Q_LORA, H * 2 * LANE)
    q = q_up_project(zqkv, w["q_norm"][j][None], wq, cos, sin)

    w_kv = jnp.concatenate([jnp.transpose(w["w_uk"][j], (1, 0, 2)).reshape(KV_LORA, H * NOPE_DIM),
                            jnp.transpose(w["w_uv"][j], (1, 0, 2)).reshape(KV_LORA, H * MLA_V_DIM)], axis=1)
    kv = matmul(c_kv, w_kv[None], rows=NP, tm=1024, tn=1024)
    o_p = flash_attention(q, kv, kv, batch=B, seq=T, heads=H, scale=MLA_SCALE, dq=2 * LANE,
                          q_col=lambda h: h, k1_col=lambda h: h, v_col=lambda h: H + h, k2=k_rope)

    q_s = q[NP:]
    qlat = head_matmul(q_s, jnp.transpose(w["w_uk"][j], (0, 2, 1)), xstride=2)
    page = cache_lat.shape[2]
    pad_new = lambda a: jnp.pad(a.reshape(DB, S, a.shape[-1]), ((0, 0), (0, page - S), (0, 0)))
    o_lat = mla_decode(page_table, qlat.reshape(DB, S * H, KV_LORA), q_s.reshape(DB, S * H, 2 * LANE),
                       pad_new(c_kv[NP:]), pad_new(k_rope[NP:]), cache_lat, jnp.swapaxes(cache_rope, 2, 3), layer=j)
    o_s = head_matmul(o_lat.reshape(NS, H * KV_LORA), w["w_uv"][j])
    o_mla = jnp.concatenate([o_p, o_s.astype(BF16)], axis=0)

    hn = w["head_norm"][j][:, None, :]
    g8 = gates[:, :2 * HM]
    hm_p, c_p, n_p, m_p = mlstm(zm, gates, jnp.transpose(g8[:NP].reshape(B, T, 2 * HM), (0, 2, 1)), hn,
                                batch=B, seq=T, chunk=MLSTM_CHUNK)
    SP = 8
    zm_s = jnp.pad(zm[NP:].reshape(DB, S, -1), ((0, 0), (0, SP - S), (0, 0))).reshape(DB * SP, -1)
    g_s = g8[NP:].reshape(DB, S, 2 * HM)
    g_s = jnp.concatenate([g_s, jnp.broadcast_to(jnp.concatenate([jnp.full((HM,), GATE_PAD, F32), jnp.zeros((HM,), F32)]),
                                                 (DB, SP - S, 2 * HM))], axis=1)
    gates_s = jnp.pad(g_s.reshape(DB * SP, 2 * HM), ((0, 0), (0, LANE - 2 * HM)))
    init = (st_c[j].reshape(DB * HM, MLSTM_DK, MLSTM_DV), st_n[j].reshape(DB * HM, 1, MLSTM_DK), st_m[j].reshape(DB * HM, 1, 1))
    hm_s, c_s, n_s, m_s = mlstm(zm_s, gates_s, jnp.transpose(g_s, (0, 2, 1)), hn, batch=DB, seq=SP, chunk=SP, init=init,
                                out_dtype=F32)
    hm = jnp.concatenate([hm_p, hm_s.reshape(DB, SP, -1)[:, :S].reshape(NS, -1).astype(BF16)], axis=0)

    h = matmul_pair(o_mla, hm, w["w_out"], layer=j)
    state = lambda c, n, m, b: (c.reshape(b, HM, MLSTM_DK, MLSTM_DV), n.reshape(b, HM, MLSTM_DK), m.reshape(b, HM))
    st_p = (c_kv[:NP].reshape(B, T, KV_LORA), k_rope[:NP, :ROPE_DIM].reshape(B, T, ROPE_DIM)) + state(c_p, n_p, m_p, B)
    st_s = (c_kv[NP:].reshape(DB, S, KV_LORA), k_rope[NP:, :ROPE_DIM].reshape(DB, S, ROPE_DIM)) + state(c_s, n_s, m_s, DB)
    return h, st_p, st_s


def _fox_layer(x_bf, dims, cache_k, cache_v, cache_f, page_table, w, j):
    B, T, DB, S = dims
    NP, NS = B * T, DB * S
    HQ, KVH, G, D = FOX_HEADS, FOX_KV_HEADS, FOX_GROUP, FOX_HEAD_DIM
    w_in = w["w_in"]
    zq = matmul(x_bf, w_in, layer=j, n_out=HQ * D)
    kvf_w = 2 * KVH * D + HQ
    w_kvf = jnp.pad(w_in[j][:, HQ * D:], ((0, 0), (0, -kvf_w % LANE)))
    zkvf = matmul(x_bf, w_kvf[None], tn=w_kvf.shape[1])
    logf = log_sigmoid_gate(zkvf, jnp.pad(w["b_f"][j], (0, LANE - HQ))[None], col=2 * KVH)[:, :HQ]

    fcum = jnp.cumsum(logf[:NP].reshape(B, T, HQ), axis=1)
    fcum_t = jnp.transpose(fcum, (0, 2, 1)).reshape(B * HQ, T) * LOG2E
    o_p = flash_attention(zq, zkvf, zkvf, batch=B, seq=T, heads=HQ, scale=FOX_SCALE, dq=D,
                          q_col=lambda h: h, k1_col=lambda h: h // G, v_col=lambda h: KVH + h // G,
                          bias_q=fcum_t[:, :, None], bias_k=fcum_t[:, None, :])

    page = cache_k.shape[2]
    n_pool = cache_k.shape[1]
    q_s = jnp.transpose(zq[NP:].reshape(DB, S, KVH, G, D), (0, 2, 1, 3, 4)).reshape(DB, KVH, S * G, D)
    f_new = jnp.cumsum(logf[NP:].reshape(DB, S, KVH, G), axis=1)
    f_new = f_new * LOG2E
    fq = jnp.transpose(f_new, (0, 2, 1, 3)).reshape(DB, KVH, S * G, 1)
    fk = jnp.transpose(f_new, (0, 2, 3, 1))
    bias_new = fq.reshape(DB, KVH, S, G, 1) - fk[:, :, None, :, :]
    causal = jnp.arange(S)[None, :] <= jnp.arange(S)[:, None]
    bias_new = jnp.where(causal[None, None, :, None, :], bias_new, NEG_INF).reshape(DB, KVH, S * G, S)
    bias_new = jnp.pad(bias_new, ((0, 0), (0, 0), (0, 0), (0, page - S)), constant_values=NEG_INF)
    kv_s = zkvf[NP:].reshape(DB, S, -1)
    pad_new = lambda a: jnp.pad(a, ((0, 0), (0, page - S), (0, 0)))
    o_s = fox_decode(page_table, q_s, fq, pad_new(kv_s[:, :, :KVH * D]), pad_new(kv_s[:, :, KVH * D:2 * KVH * D]), bias_new,
                     cache_k.reshape(cache_k.shape[0], n_pool, page * KVH, D),
                     cache_v.reshape(cache_v.shape[0], n_pool, page * KVH, D), jnp.swapaxes(cache_f, 2, 3), layer=j)
    o_s = jnp.transpose(o_s.reshape(DB, KVH, S, G, D), (0, 2, 1, 3, 4)).reshape(NS, HQ * D)
    o = jnp.concatenate([o_p, o_s.astype(BF16)], axis=0)
    h = matmul(o, w["w_out"], layer=j)

    def state(lo, hi):
        k = zkvf[lo:hi, :KVH * D]
        v = zkvf[lo:hi, KVH * D:2 * KVH * D]
        return k, v, logf[lo:hi]
    kp, vp, fp = state(0, NP)
    ks, vs, fs = state(NP, NP + NS)
    st_p = (kp.reshape(B, T, KVH, D), vp.reshape(B, T, KVH, D), fp.reshape(B, T, HQ))
    st_s = (ks.reshape(DB, S, KVH, D), vs.reshape(DB, S, KVH, D), fs.reshape(DB, S, HQ))
    return h, st_p, st_s


def kernel(x_prompt, x_sample, cache_mla_latent, cache_mla_rope, state_mlstm_c, state_mlstm_n, state_mlstm_m, cache_fox_k, cache_fox_v, cache_fox_logf, page_table, ab_w_in, ab_q_norm, ab_kv_norm, ab_w_q_up, ab_w_uk, ab_w_uv, ab_b_i, ab_b_f, ab_head_norm, ab_w_out, fox_w_in, fox_b_f, fox_w_out, ln1_g, ln1_b, ln2_g, ln2_b, moe_w_group, moe_b_group, moe_w_router, moe_b_router, moe_w_gate, moe_w_up, moe_w_down):
    B, T, D = x_prompt.shape
    DB, S, _ = x_sample.shape
    NP = B * T
    dims = (B, T, DB, S)
    depth = ln1_g.shape[0]
    alpha = (2.0 * depth) ** 0.25
    x = jnp.concatenate([x_prompt.reshape(NP, D), x_sample.reshape(DB * S, D)], axis=0)
    x_bf = x.astype(BF16)
    ab = {"w_in": ab_w_in, "q_norm": ab_q_norm, "kv_norm": ab_kv_norm, "w_q_up": ab_w_q_up, "w_uk": ab_w_uk,
          "w_uv": ab_w_uv, "b_i": ab_b_i, "b_f": ab_b_f, "head_norm": ab_head_norm, "w_out": ab_w_out}
    fox = {"w_in": fox_w_in, "b_f": fox_b_f, "w_out": fox_w_out}
    ln = [a[:, None, :] for a in (ln1_g, ln1_b, ln2_g, ln2_b)]
    ab_p, ab_s, fx_p, fx_s = [], [], [], []
    for layer in range(depth):
        j = layer // 2
        if layer % 2 == 0:
            h, st_p, st_s = _ab_layer(x, x_bf, dims, cache_mla_latent, cache_mla_rope, state_mlstm_c, state_mlstm_n,
                                      state_mlstm_m, page_table, ab, j)
            ab_p.append(st_p)
            ab_s.append(st_s)
        else:
            h, st_p, st_s = _fox_layer(x_bf, dims, cache_fox_k, cache_fox_v, cache_fox_logf, page_table, fox, j)
            fx_p.append(st_p)
            fx_s.append(st_s)
        x, x_bf = deepnorm_ln(x, (h,), ln[0], ln[1], layer=layer, alpha=alpha)
        hs = hier_moe(x, x_bf, moe_w_group, moe_b_group, moe_w_router, moe_b_router, moe_w_gate, moe_w_up, moe_w_down,
                      layer=layer)
        x, x_bf = deepnorm_ln(x, hs, ln[2], ln[3], layer=layer, alpha=alpha)

    def stack(group, idx, like):
        return jnp.stack([st[idx] for st in group]).astype(like.dtype)

    caches_ab = (cache_mla_latent, cache_mla_rope, state_mlstm_c, state_mlstm_n, state_mlstm_m)
    caches_fx = (cache_fox_k, cache_fox_v, cache_fox_logf)
    out = [x[:NP].reshape(B, T, D), x[NP:].reshape(DB, S, D)]
    out += [stack(ab_p, i, c) for i, c in enumerate(caches_ab)]
    out += [stack(fx_p, i, c) for i, c in enumerate(caches_fx)]
    out += [stack(ab_s, i, c) for i, c in enumerate(caches_ab)]
    out += [stack(fx_s, i, c) for i, c in enumerate(caches_fx)]
    return tuple(out)
```
